```python
import jax, jax.numpy as jnp
from jax import lax
import numpy as np

D_MODEL = 1024
BATCH = 32
SEQ = 2048
DEPTH = 4
DEC_BATCH = 32
DEC_SEQ = 64
PAST_LEN = 1024

CHUNK = 64
D_MIX = D_MODEL
D_CONV = D_MIX // 2
CONV_GROUPS = 8
CONV_W = 3
D_GLA = D_MIX - D_CONV
N_HEADS = 4
HEAD_V = D_GLA // N_HEADS
HEAD_K = HEAD_V // 2
D_QK = N_HEADS * HEAD_K
GATE_RANK = 16
GATE_TAU = 16.0
D_FF = 2816
FFN_CONV_W = 3
EPS = 1e-6
D_IN_PROJ = 3 * D_CONV + 2 * D_QK + D_GLA + GATE_RANK + D_GLA
SPLITS = list(np.cumsum([D_CONV, D_CONV, D_CONV, D_QK, D_QK, D_GLA, GATE_RANK])[:].tolist())

kernel_name = "hybrid_conv_gla_streaming_encoder_step"


def rmsnorm(x, g):
    xf = x.astype(jnp.float32)
    y = xf * lax.rsqrt(jnp.mean(xf * xf, axis=-1, keepdims=True) + EPS) * g.astype(jnp.float32)
    return y.astype(x.dtype)


def causal_dwconv(u, prev, w):
    width = w.shape[0]
    L = u.shape[1]
    up = jnp.concatenate([prev.astype(u.dtype), u], axis=1)
    y = up[:, 0:L] * w[0]
    for t in range(1, width):
        y = y + up[:, t:t + L] * w[t]
    return y, up[:, L:]


def gla(q, k, v, logg, s0):
    Bsz, L, H, _ = q.shape
    V = v.shape[-1]
    C = min(CHUNK, L)
    N = L // C
    f32 = jnp.float32

    def blk(t):
        return t.astype(f32).reshape(Bsz, N, C, H, t.shape[-1]).transpose(1, 0, 3, 2, 4)

    q, k, v, logg = blk(q), blk(k), blk(v), blk(logg)
    b = jnp.cumsum(logg, axis=3)
    b_last = b[:, :, :, -1:, :]
    qt = q * jnp.exp(b)
    kt = k * jnp.exp(-b)
    kd = k * jnp.exp(b_last - b)
    mask = jnp.tril(jnp.ones((C, C), dtype=bool))
    att = jnp.where(mask, jnp.einsum('nbhik,nbhjk->nbhij', qt, kt), 0.0)
    o_intra = jnp.einsum('nbhij,nbhjv->nbhiv', att, v)
    ds = jnp.einsum('nbhck,nbhcv->nbhkv', kd, v)
    decay = jnp.exp(b_last[:, :, :, 0, :])

    def step(s, inp):
        d, dsn = inp
        return d[..., None] * s + dsn, s

    s_final, s_start = lax.scan(step, s0.astype(f32), (decay, ds))
    o = o_intra + jnp.einsum('nbhik,nbhkv->nbhiv', qt, s_start)
    o = o.transpose(1, 0, 3, 2, 4).reshape(Bsz, L, H, V)
    return o, s_final.astype(s0.dtype)


def mixer(xn, conv_prev, gla_prev, w_in, w_alpha, b_alpha, conv_w, g_conv, g_gla, w_out):
    Bsz, L, _ = xn.shape
    z = xn @ w_in
    bg, cg, h, q, k, v, gr, r = jnp.split(z, SPLITS, axis=-1)
    cv, conv_new = causal_dwconv(cg * h, conv_prev, conv_w)
    y_conv = rmsnorm(bg * cv, g_conv)
    logg = jax.nn.log_sigmoid((gr @ w_alpha + b_alpha).astype(jnp.float32)) / GATE_TAU
    q = q.reshape(Bsz, L, N_HEADS, HEAD_K) * (HEAD_K ** -0.5)
    k = k.reshape(Bsz, L, N_HEADS, HEAD_K)
    v = v.reshape(Bsz, L, N_HEADS, HEAD_V)
    logg = logg.reshape(Bsz, L, N_HEADS, HEAD_K)
    o, gla_new = gla(q, k, v, logg, gla_prev)
    o = rmsnorm(o, g_gla).astype(xn.dtype).reshape(Bsz, L, D_GLA)
    y_gla = o * jax.nn.silu(r)
    y = jnp.concatenate([y_conv, y_gla], axis=-1) @ w_out
    return y, conv_new, gla_new


def conv_ffn(xn, prev, w_up, conv_w, w_down):
    u = xn @ w_up
    uc, new = causal_dwconv(u, prev, conv_w)
    a, g = jnp.split(uc, 2, axis=-1)
    return (jax.nn.silu(g) * a) @ w_down, new


def trunk(x, conv_s, gla_s, ffn_s, w_in, w_alpha, b_alpha, conv_w, g_conv, g_gla, w_out,
          norm1, norm2, w_up, ffn_conv_w, w_down, final_norm):
    new_c, new_g, new_f = [], [], []
    for l in range(DEPTH):
        y, c, g = mixer(rmsnorm(x, norm1[l]), conv_s[l], gla_s[l], w_in[l], w_alpha[l], b_alpha[l],
                        conv_w[l], g_conv[l], g_gla[l], w_out[l])
        x = x + y
        y, f = conv_ffn(rmsnorm(x, norm2[l]), ffn_s[l], w_up[l], ffn_conv_w[l], w_down[l])
        x = x + y
        new_c.append(c)
        new_g.append(g)
        new_f.append(f)
    return rmsnorm(x, final_norm), jnp.stack(new_c), jnp.stack(new_g), jnp.stack(new_f)


def setup_inputs(seed: int = 0) -> dict:
    key = jax.random.key(seed)
    ks = jax.random.split(key, 20)
    f32 = jnp.float32
    nrm = lambda k, s, sc: jax.random.normal(k, s, f32) * sc
    return {
        "x_prompt": nrm(ks[0], (BATCH, SEQ, D_MODEL), 1.0),
        "x_sample": nrm(ks[1], (DEC_BATCH, DEC_SEQ, D_MODEL), 1.0),
        "state_conv": nrm(ks[2], (DEPTH, DEC_BATCH, CONV_W - 1, D_CONV), 1.0),
        "state_gla": nrm(ks[3], (DEPTH, DEC_BATCH, N_HEADS, HEAD_K, HEAD_V), 1.0),
        "state_ffn_conv": nrm(ks[4], (DEPTH, DEC_BATCH, FFN_CONV_W - 1, 2 * D_FF), 1.0),
        "w_in": nrm(ks[5], (DEPTH, D_MODEL, D_IN_PROJ), D_MODEL ** -0.5),
        "w_alpha": nrm(ks[6], (DEPTH, GATE_RANK, D_QK), GATE_RANK ** -0.5),
        "b_alpha": nrm(ks[7], (DEPTH, D_QK), 0.01),
        "conv_w": nrm(ks[8], (DEPTH, CONV_W, D_CONV), CONV_W ** -0.5),
        "g_conv": 1.0 + nrm(ks[9], (DEPTH, D_CONV), 0.02),
        "g_gla": 1.0 + nrm(ks[10], (DEPTH, HEAD_V), 0.02),
        "w_out": nrm(ks[11], (DEPTH, D_MIX, D_MODEL), D_MIX ** -0.5),
        "norm1": 1.0 + nrm(ks[12], (DEPTH, D_MODEL), 0.02),
        "norm2": 1.0 + nrm(ks[13], (DEPTH, D_MODEL), 0.02),
        "w_up": nrm(ks[14], (DEPTH, D_MODEL, 2 * D_FF), D_MODEL ** -0.5),
        "ffn_conv_w": nrm(ks[15], (DEPTH, FFN_CONV_W, 2 * D_FF), FFN_CONV_W ** -0.5),
        "w_down": nrm(ks[16], (DEPTH, D_FF, D_MODEL), D_FF ** -0.5),
        "final_norm": 1.0 + nrm(ks[17], (D_MODEL,), 0.02),
    }


def reference(x_prompt, x_sample, state_conv, state_gla, state_ffn_conv, w_in, w_alpha, b_alpha,
              conv_w, g_conv, g_gla, w_out, norm1, norm2, w_up, ffn_conv_w, w_down, final_norm):
    weights = (w_in, w_alpha, b_alpha, conv_w, g_conv, g_gla, w_out,
               norm1, norm2, w_up, ffn_conv_w, w_down, final_norm)
    bp = x_prompt.shape[0]
    dt = x_prompt.dtype
    zc = jnp.zeros((DEPTH, bp, CONV_W - 1, D_CONV), dt)
    zg = jnp.zeros((DEPTH, bp, N_HEADS, HEAD_K, HEAD_V), dt)
    zf = jnp.zeros((DEPTH, bp, FFN_CONV_W - 1, 2 * D_FF), dt)
    y_prompt, pc, pg, pf = trunk(x_prompt, zc, zg, zf, *weights)
    y_sample, sc, sg, sf = trunk(x_sample, state_conv, state_gla, state_ffn_conv, *weights)
    return (y_prompt, y_sample, pc, pg, pf, sc, sg, sf)
```

```python
import functools

import jax
import jax.numpy as jnp
from jax import lax
from jax.experimental import pallas as pl
from jax.experimental.pallas import tpu as pltpu

F32 = jnp.float32
BF16 = jnp.bfloat16

D_MODEL = 1024
D_CONV = 512
N_HEADS = 4
HEAD_K = 64
HEAD_V = 128
D_QK = N_HEADS * HEAD_K
D_GLA = N_HEADS * HEAD_V
GATE_RANK = 16
GATE_TAU = 16.0
D_FF = 2816
CHUNK = 64
EPS = 1e-6

LANES = 128
C_BG, C_CG, C_H = 0, D_CONV, 2 * D_CONV
C_Q = 3 * D_CONV
C_K = C_Q + D_QK
C_V = C_K + D_QK
C_R = C_V + D_GLA
C_GR = C_R + D_GLA
D_IN_PAD = C_GR + LANES

FFN_COL_BLOCK = 256
VMEM_LIMIT_BYTES = 56 * 1024 * 1024


def _rms_scale(v):
    return lax.rsqrt(jnp.mean(v * v, axis=-1, keepdims=True) + EPS)


def _log_sigmoid(v):
    return jnp.minimum(v, 0.0) - jnp.log1p(jnp.exp(-jnp.abs(v)))


def _causal_taps(u, prev, w):
    t = u.shape[0]
    row = lax.broadcasted_iota(jnp.int32, (t, 1), 0)
    s1 = jnp.where(row == 0, prev[1:2], pltpu.roll(u, 1, 0))
    s2 = jnp.where(row == 0, prev[0:1], jnp.where(row == 1, prev[1:2], pltpu.roll(u, 2, 0)))
    return s2 * w[0:1] + s1 * w[1:2] + u * w[2:3]


def _mixer_kernel(x_ref, conv0_ref, gla0_ref, w_in_ref, w_alpha_ref, b_alpha_ref, conv_w_ref,
                  g_conv_ref, g_gla_ref, w_out_ref, norm_ref,
                  y_ref, conv_ref, gla_ref,
                  z_ref, qt_ref, kt_ref, kd_ref, v_ref, dec_ref, o_ref, ycat_ref, st_ref,
                  *, bt, t):
    m = bt * t
    cps = t // CHUNK
    nc = m // CHUNK
    s = pl.program_id(1)
    last = pl.num_programs(1) - 1

    kv_mask = (lax.broadcasted_iota(jnp.int32, (D_QK, D_GLA), 0) // HEAD_K
               == lax.broadcasted_iota(jnp.int32, (D_QK, D_GLA), 1) // HEAD_V)
    vk_mask = (lax.broadcasted_iota(jnp.int32, (D_GLA, D_QK), 0) // HEAD_V
               == lax.broadcasted_iota(jnp.int32, (D_GLA, D_QK), 1) // HEAD_K)
    kk_mask = (lax.broadcasted_iota(jnp.int32, (D_QK, D_QK), 0) // HEAD_K
               == lax.broadcasted_iota(jnp.int32, (D_QK, D_QK), 1) // HEAD_K)
    causal = (lax.broadcasted_iota(jnp.int32, (CHUNK, D_QK), 1) % CHUNK
              <= lax.broadcasted_iota(jnp.int32, (CHUNK, D_QK), 0))

    @pl.when(s == 0)
    def _():
        conv_ref[...] = conv0_ref[...]
        for b in range(bt):
            s0 = gla0_ref[b].reshape(D_QK, HEAD_V)
            s0 = jnp.where(kv_mask, jnp.concatenate([s0] * N_HEADS, axis=1), 0.0)
            st_ref[b] = s0.T

    x = x_ref[...].reshape(m, D_MODEL)
    xn = x * _rms_scale(x) * norm_ref[...]
    z_ref[...] = jnp.dot(xn.astype(BF16), w_in_ref[...], preferred_element_type=F32)

    for b in range(bt):
        rows = slice(b * t, (b + 1) * t)
        ch = z_ref[rows, C_CG:C_CG + D_CONV] * z_ref[rows, C_H:C_H + D_CONV]
        cv = _causal_taps(ch, conv_ref[b], conv_w_ref[...])
        conv_ref[b] = ch[t - 2:t]
        yc = z_ref[rows, C_BG:C_BG + D_CONV] * cv
        ycat_ref[rows, 0:D_CONV] = (yc * _rms_scale(yc) * g_conv_ref[...]).astype(BF16)

    gate = jnp.dot(z_ref[:, C_GR:C_GR + LANES].astype(BF16), w_alpha_ref[...],
                   preferred_element_type=F32) + b_alpha_ref[...]
    bcum = _log_sigmoid(gate) * (1.0 / GATE_TAU)
    pos = lax.broadcasted_iota(jnp.int32, (m, 1), 0) % CHUNK
    shift = 1
    while shift < CHUNK:
        bcum = bcum + jnp.where(pos >= shift, pltpu.roll(bcum, shift, 0), 0.0)
        shift *= 2
    b3 = bcum.reshape(nc, CHUNK, D_QK)
    b_last = b3[:, CHUNK - 1:CHUNK, :]
    q = z_ref[:, C_Q:C_Q + D_QK] * (HEAD_K ** -0.5)
    k = z_ref[:, C_K:C_K + D_QK]
    qt_ref[...] = (q * jnp.exp(bcum)).astype(BF16)
    kt_ref[...] = (k * jnp.exp(-bcum)).astype(BF16)
    kd_ref[...] = (k.reshape(nc, CHUNK, D_QK) * jnp.exp(b_last - b3)).reshape(m, D_QK).astype(BF16)
    dec_ref[...] = jnp.exp(b_last)
    v_ref[...] = z_ref[:, C_V:C_V + D_GLA].astype(BF16)

    def chunk_step(c, carry):
        b = c // cps
        rows = pl.ds(pl.multiple_of(c * CHUNK, CHUNK), CHUNK)
        qt = qt_ref[rows, :]
        kt = kt_ref[rows, :]
        kd = kd_ref[rows, :]
        v = v_ref[rows, :]
        st = st_ref[b]
        k_bd = jnp.where(kk_mask, jnp.concatenate([kt] * N_HEADS, axis=0), 0.0)
        att = lax.dot_general(qt, k_bd, (((1,), (1,)), ((), ())), preferred_element_type=F32)
        att = jnp.where(causal, att, 0.0)
        v_bd = jnp.where(kv_mask, jnp.concatenate([v] * N_HEADS, axis=0), 0.0)
        o = jnp.dot(att.astype(BF16), v_bd, preferred_element_type=F32)
        o = o + lax.dot_general(qt, st.astype(BF16), (((1,), (1,)), ((), ())),
                                preferred_element_type=F32)
        o_ref[rows, :] = o
        ds_t = lax.dot_general(v, kd, (((0,), (0,)), ((), ())), preferred_element_type=F32)
        st_ref[b] = st * dec_ref[c] + jnp.where(vk_mask, ds_t, 0.0)
        return carry

    lax.fori_loop(0, nc, chunk_step, 0)

    for h in range(N_HEADS):
        o = o_ref[:, h * HEAD_V:(h + 1) * HEAD_V]
        r = z_ref[:, C_R + h * HEAD_V:C_R + (h + 1) * HEAD_V]
        o = o * _rms_scale(o) * g_gla_ref[...]
        ycat_ref[:, D_CONV + h * HEAD_V:D_CONV + (h + 1) * HEAD_V] = (
            o * (r * jax.nn.sigmoid(r))).astype(BF16)
    y = jnp.dot(ycat_ref[...], w_out_ref[...], preferred_element_type=F32)
    y_ref[...] = (x_ref[...].reshape(m, D_MODEL) + y).reshape(bt, t, D_MODEL)

    @pl.when(s == last)
    def _():
        for b in range(bt):
            full = st_ref[b].T
            for h in range(N_HEADS):
                gla_ref[b, h] = full[h * HEAD_K:(h + 1) * HEAD_K, h * HEAD_V:(h + 1) * HEAD_V]


def _ffn_kernel(x_ref, st0_ref, w_up_ref, conv_w_ref, w_down_ref, norm_ref, fnorm_ref,
                y_ref, st_ref, xn_ref, hid_ref, *, bt, t, final):
    m = bt * t
    s = pl.program_id(1)

    @pl.when(s == 0)
    def _():
        st_ref[...] = st0_ref[...]

    x = x_ref[...].reshape(m, D_MODEL)
    xn_ref[...] = (x * _rms_scale(x) * norm_ref[...]).astype(BF16)

    for j in range(D_FF // FFN_COL_BLOCK):
        halves = []
        for base in (0, D_FF):
            cols = slice(base + j * FFN_COL_BLOCK, base + (j + 1) * FFN_COL_BLOCK)
            u = jnp.dot(xn_ref[...], w_up_ref[:, cols], preferred_element_type=F32)
            w = conv_w_ref[:, cols]
            parts = []
            for b in range(bt):
                ub = u[b * t:(b + 1) * t]
                parts.append(_causal_taps(ub, st_ref[b, :, cols], w))
                st_ref[b, :, cols] = ub[t - 2:t]
            halves.append(parts[0] if bt == 1 else jnp.concatenate(parts, axis=0))
        a, g = halves
        hid_ref[:, j * FFN_COL_BLOCK:(j + 1) * FFN_COL_BLOCK] = (
            g * jax.nn.sigmoid(g) * a).astype(BF16)

    y = x_ref[...].reshape(m, D_MODEL) + jnp.dot(hid_ref[...], w_down_ref[...],
                                                  preferred_element_type=F32)
    if final:
        y = y * _rms_scale(y) * fnorm_ref[...]
    y_ref[...] = y.reshape(bt, t, D_MODEL)


def _const_spec(shape):
    return pl.BlockSpec(shape, lambda b, s: (0,) * len(shape))


def _block_sizes(batch, length):
    t = min(length, 512)
    bt = max(1, min(batch, 256 // t))
    assert length % t == 0 and batch % bt == 0 and t % CHUNK == 0
    return bt, t


def _mixer_call(x, conv0, gla0, w_in, w_alpha, b_alpha, conv_w, g_conv, g_gla, w_out, norm):
    batch, length, _ = x.shape
    bt, t = _block_sizes(batch, length)
    m = bt * t
    nc = m // CHUNK
    x_spec = pl.BlockSpec((bt, t, D_MODEL), lambda b, s: (b, s, 0))
    conv_spec = pl.BlockSpec((bt, 2, D_CONV), lambda b, s: (b, 0, 0))
    gla_spec = pl.BlockSpec((bt, N_HEADS, HEAD_K, HEAD_V), lambda b, s: (b, 0, 0, 0))
    return pl.pallas_call(
        functools.partial(_mixer_kernel, bt=bt, t=t),
        grid=(batch // bt, length // t),
        in_specs=[x_spec, conv_spec, gla_spec,
                  _const_spec(w_in.shape), _const_spec(w_alpha.shape), _const_spec(b_alpha.shape),
                  _const_spec(conv_w.shape), _const_spec(g_conv.shape), _const_spec(g_gla.shape),
                  _const_spec(w_out.shape), _const_spec(norm.shape)],
        out_specs=[x_spec, conv_spec, gla_spec],
        out_shape=[jax.ShapeDtypeStruct(x.shape, F32),
                   jax.ShapeDtypeStruct(conv0.shape, F32),
                   jax.ShapeDtypeStruct(gla0.shape, F32)],
        scratch_shapes=[
            pltpu.VMEM((m, D_IN_PAD), F32),
            pltpu.VMEM((m, D_QK), BF16),
            pltpu.VMEM((m, D_QK), BF16),
            pltpu.VMEM((m, D_QK), BF16),
            pltpu.VMEM((m, D_GLA), BF16),
            pltpu.VMEM((nc, 1, D_QK), F32),
            pltpu.VMEM((m, D_GLA), F32),
            pltpu.VMEM((m, D_MODEL), BF16),
            pltpu.VMEM((bt, D_GLA, D_QK), F32),
        ],
        compiler_params=pltpu.CompilerParams(
            dimension_semantics=("arbitrary", "arbitrary"),
            vmem_limit_bytes=VMEM_LIMIT_BYTES),
        name="mixer",
    )(x, conv0, gla0, w_in, w_alpha, b_alpha, conv_w, g_conv, g_gla, w_out, norm)


def _ffn_call(x, st0, w_up, conv_w, w_down, norm, fnorm, final):
    batch, length, _ = x.shape
    bt, t = _block_sizes(batch, length)
    m = bt * t
    x_spec = pl.BlockSpec((bt, t, D_MODEL), lambda b, s: (b, s, 0))
    st_spec = pl.BlockSpec((bt, 2, 2 * D_FF), lambda b, s: (b, 0, 0))
    return pl.pallas_call(
        functools.partial(_ffn_kernel, bt=bt, t=t, final=final),
        grid=(batch // bt, length // t),
        in_specs=[x_spec, st_spec, _const_spec(w_up.shape), _const_spec(conv_w.shape),
                  _const_spec(w_down.shape), _const_spec(norm.shape), _const_spec(fnorm.shape)],
        out_specs=[x_spec, st_spec],
        out_shape=[jax.ShapeDtypeStruct(x.shape, F32), jax.ShapeDtypeStruct(st0.shape, F32)],
        scratch_shapes=[pltpu.VMEM((m, D_MODEL), BF16),
                        pltpu.VMEM((m, D_FF), BF16)],
        compiler_params=pltpu.CompilerParams(
            dimension_semantics=("arbitrary", "arbitrary"),
            vmem_limit_bytes=VMEM_LIMIT_BYTES),
        name="ffn",
    )(x, st0, w_up, conv_w, w_down, norm, fnorm)


def _trunk(x, conv_s, gla_s, ffn_s, p):
    depth = conv_s.shape[0]
    new_c, new_g, new_f = [], [], []
    for l in range(depth):
        x, c, g = _mixer_call(x, conv_s[l], gla_s[l], p["w_in"][l], p["w_alpha"][l], p["b_alpha"][l],
                              p["conv_w"][l], p["g_conv"][l], p["g_gla"][l], p["w_out"][l],
                              p["norm1"][l])
        x, f = _ffn_call(x, ffn_s[l], p["w_up"][l], p["ffn_conv_w"][l], p["w_down"][l],
                         p["norm2"][l], p["final_norm"], final=(l == depth - 1))
        new_c.append(c)
        new_g.append(g)
        new_f.append(f)
    return x, jnp.stack(new_c), jnp.stack(new_g), jnp.stack(new_f)


def kernel(x_prompt, x_sample, state_conv, state_gla, state_ffn_conv, w_in, w_alpha, b_alpha, conv_w, g_conv, g_gla, w_out, norm1, norm2, w_up, ffn_conv_w, w_down, final_norm):
    depth = w_in.shape[0]
    gate_cols = w_in[:, :, C_R:C_R + GATE_RANK]
    w_in_r = jnp.concatenate(
        [w_in[:, :, :C_R], w_in[:, :, C_R + GATE_RANK:],
         jnp.pad(gate_cols, ((0, 0), (0, 0), (0, LANES - GATE_RANK)))], axis=-1).astype(BF16)
    w_alpha_p = jnp.pad(w_alpha, ((0, 0), (0, LANES - GATE_RANK), (0, 0))).astype(BF16)
    p = {
        "w_in": w_in_r, "w_alpha": w_alpha_p,
        "b_alpha": b_alpha.reshape(depth, 1, D_QK),
        "conv_w": conv_w, "g_conv": g_conv.reshape(depth, 1, D_CONV),
        "g_gla": g_gla.reshape(depth, 1, HEAD_V),
        "w_out": w_out.astype(BF16),
        "norm1": norm1.reshape(depth, 1, D_MODEL), "norm2": norm2.reshape(depth, 1, D_MODEL),
        "w_up": w_up.astype(BF16), "ffn_conv_w": ffn_conv_w, "w_down": w_down.astype(BF16),
        "final_norm": final_norm.reshape(1, D_MODEL),
    }
    bp = x_prompt.shape[0]
    dt = x_prompt.dtype
    zc = jnp.zeros((depth, bp) + state_conv.shape[2:], dt)
    zg = jnp.zeros((depth, bp) + state_gla.shape[2:], dt)
    zf = jnp.zeros((depth, bp) + state_ffn_conv.shape[2:], dt)
    y_prompt, pc, pg, pf = _trunk(x_prompt, zc, zg, zf, p)
    y_sample, sc, sg, sf = _trunk(x_sample, state_conv, state_gla, state_ffn_conv, p)
    return (y_prompt, y_sample, pc, pg, pf, sc, sg, sf)
```

```python
import functools

import jax
import jax.numpy as jnp
from jax import lax
from jax.experimental import pallas as pl
from jax.experimental.pallas import tpu as pltpu

F32 = jnp.float32
BF16 = jnp.bfloat16

D_MODEL = 1024
D_CONV = 512
N_HEADS = 4
HEAD_K = 64
HEAD_V = 128
D_QK = N_HEADS * HEAD_K
D_GLA = N_HEADS * HEAD_V
GATE_RANK = 16
GATE_TAU = 16.0
D_FF = 2816
CHUNK = 64
EPS = 1e-6

LANES = 128
C_BG, C_CG, C_H = 0, D_CONV, 2 * D_CONV
C_Q = 3 * D_CONV
C_K = C_Q + D_QK
C_V = C_K + D_QK
C_R = C_V + D_GLA
C_GR = C_R + D_GLA
D_IN_PAD = C_GR + LANES

FFN_COL_BLOCK = 256
FFN_TIME_BLOCK = 512
MIXER_TIME_BLOCK = 1024
MIXER_ROW_BLOCK = 256
MIXER_PROJ_COLS = 512
VMEM_LIMIT_BYTES = 56 * 1024 * 1024


def _rms_scale(v):
    return lax.rsqrt(jnp.mean(v * v, axis=-1, keepdims=True) + EPS)


def _log_sigmoid(v):
    return jnp.minimum(v, 0.0) - jnp.log1p(jnp.exp(-jnp.abs(v)))


def _causal_taps(u, prev, w):
    t = u.shape[0]
    row = lax.broadcasted_iota(jnp.int32, (t, 1), 0)
    s1 = jnp.where(row == 0, prev[1:2], pltpu.roll(u, 1, 0))
    s2 = jnp.where(row == 0, prev[0:1], jnp.where(row == 1, prev[1:2], pltpu.roll(u, 2, 0)))
    return s2 * w[0:1] + s1 * w[1:2] + u * w[2:3]


def _half_lane_mask(rows, head):
    lane = lax.broadcasted_iota(jnp.int32, (rows, LANES), 1)
    return (lane // HEAD_K) == (head % 2)


def _block_rows(ref, r0, n, t):
    if t >= n:
        return ref[r0 // t, r0 % t:r0 % t + n, :]
    return ref[r0 // t:(r0 + n) // t].reshape(n, ref.shape[-1])


def _store_block_rows(ref, r0, n, t, val):
    if t >= n:
        ref[r0 // t, r0 % t:r0 % t + n, :] = val
    else:
        ref[r0 // t:(r0 + n) // t] = val.reshape(n // t, t, ref.shape[-1])


def _mixer_kernel(x_ref, conv0_ref, gla0_ref, w_in_ref, w_alpha_ref, b_alpha_ref, conv_w_ref,
                  g_conv_ref, g_gla_ref, w_out_ref, norm_ref,
                  y_ref, conv_ref, gla_ref,
                  xn_ref, z_ref, qt_ref, kt_ref, kd_ref, v_ref, o_ref, ycat_ref, st_ref,
                  *, bt, t, rb):
    m = bt * t
    ncs = rb // CHUNK
    s = pl.program_id(1)
    last = pl.num_programs(1) - 1

    @pl.when(s == 0)
    def _():
        conv_ref[...] = conv0_ref[...]
        kv_mask = (lax.broadcasted_iota(jnp.int32, (D_QK, D_GLA), 0) // HEAD_K
                   == lax.broadcasted_iota(jnp.int32, (D_QK, D_GLA), 1) // HEAD_V)
        for b in range(bt):
            s0 = gla0_ref[b].reshape(D_QK, HEAD_V)
            s0 = jnp.where(kv_mask, jnp.concatenate([s0] * N_HEADS, axis=1), 0.0)
            st_ref[b] = s0.T

    causal = (lax.broadcasted_iota(jnp.int32, (CHUNK, D_QK), 1) % CHUNK
              <= lax.broadcasted_iota(jnp.int32, (CHUNK, D_QK), 0))
    zero_kk = jnp.zeros((CHUNK, LANES), BF16)
    zero_vv = jnp.zeros((CHUNK, HEAD_V), BF16)
    zero_st = jnp.zeros((HEAD_V, LANES), BF16)

    def projection_pieces(r0):
        slot = (r0 // rb) % 2
        x = _block_rows(x_ref, r0, rb, t)
        xn_ref[slot] = (x * _rms_scale(x) * norm_ref[...]).astype(BF16)

        def piece(c0):
            c1 = min(c0 + MIXER_PROJ_COLS, D_IN_PAD)
            z_ref[slot, :, c0:c1] = jnp.dot(xn_ref[slot], w_in_ref[:, c0:c1],
                                            preferred_element_type=F32)
        return [functools.partial(piece, c0) for c0 in range(0, D_IN_PAD, MIXER_PROJ_COLS)]

    def run(pieces, n):
        for _ in range(min(n, len(pieces))):
            pieces.pop(0)()

    first = projection_pieces(0)
    run(first, len(first))
    for r0 in range(0, m, rb):
        slot = (r0 // rb) % 2
        z = z_ref.at[slot]
        qt_s, kt_s, kd_s, v_s = qt_ref.at[slot], kt_ref.at[slot], kd_ref.at[slot], v_ref.at[slot]
        o_s, ycat_s = o_ref.at[slot], ycat_ref.at[slot]
        ahead = projection_pieces(r0 + rb) if r0 + rb < m else []

        gate = jnp.dot(z[:, C_GR:C_GR + LANES].astype(BF16), w_alpha_ref[...],
                       preferred_element_type=F32) + b_alpha_ref[...]
        run(ahead, 2)

        piece = min(t, rb)
        for p0 in range(0, rb, piece):
            b = (r0 + p0) // t
            prow = slice(p0, p0 + piece)
            ch = z[prow, C_CG:C_CG + D_CONV] * z[prow, C_H:C_H + D_CONV]
            cv = _causal_taps(ch, conv_ref[b], conv_w_ref[...])
            conv_ref[b] = ch[piece - 2:piece]
            yc = z[prow, C_BG:C_BG + D_CONV] * cv
            ycat_s[prow, 0:D_CONV] = (yc * _rms_scale(yc) * g_conv_ref[...]).astype(BF16)

        bcum = _log_sigmoid(gate) * (1.0 / GATE_TAU)
        pos = lax.broadcasted_iota(jnp.int32, (rb, 1), 0) % CHUNK
        shift = 1
        while shift < CHUNK:
            bcum = bcum + jnp.where(pos >= shift, pltpu.roll(bcum, shift, 0), 0.0)
            shift *= 2
        b3 = bcum.reshape(ncs, CHUNK, D_QK)
        b_last = b3[:, CHUNK - 1:CHUNK, :]
        q = z[:, C_Q:C_Q + D_QK] * (HEAD_K ** -0.5)
        k = z[:, C_K:C_K + D_QK]
        qt_s[...] = (q * jnp.exp(bcum)).astype(BF16)
        kt_s[...] = (k * jnp.exp(-bcum)).astype(BF16)
        kd_s[...] = (k.reshape(ncs, CHUNK, D_QK)
                     * jnp.exp(b_last - b3)).reshape(rb, D_QK).astype(BF16)
        dec = jnp.exp(b_last)
        v_s[...] = z[:, C_V:C_V + D_GLA].astype(BF16)

        for c in range(ncs):
            b = (r0 + c * CHUNK) // t
            crow = slice(c * CHUNK, (c + 1) * CHUNK)
            qt = qt_s[crow, :]
            kt = kt_s[crow, :]
            kd = kd_s[crow, :]
            v = v_s[crow, :]
            k_bd = jnp.concatenate([
                jnp.concatenate([
                    jnp.where(_half_lane_mask(CHUNK, h), kt[:, g * LANES:(g + 1) * LANES], 0.0)
                    if g == h // 2 else zero_kk for g in range(D_QK // LANES)], axis=1)
                for h in range(N_HEADS)], axis=0)
            att = lax.dot_general(qt, k_bd, (((1,), (1,)), ((), ())), preferred_element_type=F32)
            ds_t = lax.dot_general(v, kd, (((0,), (0,)), ((), ())), preferred_element_type=F32)
            run(ahead, 1)
            att = jnp.where(causal, att, 0.0)
            v_bd = jnp.concatenate([
                jnp.concatenate([v[:, h * HEAD_V:(h + 1) * HEAD_V] if g == h else zero_vv
                                 for g in range(N_HEADS)], axis=1)
                for h in range(N_HEADS)], axis=0)
            st_blocks = [st_ref[b, h * HEAD_V:(h + 1) * HEAD_V,
                                (h // 2) * LANES:(h // 2 + 1) * LANES] for h in range(N_HEADS)]
            st_bf = jnp.concatenate([
                jnp.concatenate([st_blocks[h].astype(BF16) if g == h // 2 else zero_st
                                 for g in range(D_QK // LANES)], axis=1)
                for h in range(N_HEADS)], axis=0)
            o = jnp.dot(att.astype(BF16), v_bd, preferred_element_type=F32)
            o = o + lax.dot_general(qt, st_bf, (((1,), (1,)), ((), ())),
                                    preferred_element_type=F32)
            o_s[crow, :] = o
            for h in range(N_HEADS):
                g = h // 2
                upd = jnp.where(_half_lane_mask(HEAD_V, h),
                                ds_t[h * HEAD_V:(h + 1) * HEAD_V, g * LANES:(g + 1) * LANES], 0.0)
                st_ref[b, h * HEAD_V:(h + 1) * HEAD_V, g * LANES:(g + 1) * LANES] = (
                    st_blocks[h] * dec[c, :, g * LANES:(g + 1) * LANES] + upd)

        run(ahead, len(ahead))
        for h in range(N_HEADS):
            o = o_s[:, h * HEAD_V:(h + 1) * HEAD_V]
            r = z[:, C_R + h * HEAD_V:C_R + (h + 1) * HEAD_V]
            o = o * _rms_scale(o) * g_gla_ref[...]
            ycat_s[:, D_CONV + h * HEAD_V:D_CONV + (h + 1) * HEAD_V] = (
                o * (r * jax.nn.sigmoid(r))).astype(BF16)
        y = jnp.dot(ycat_s[...], w_out_ref[...], preferred_element_type=F32)
        _store_block_rows(y_ref, r0, rb, t, _block_rows(x_ref, r0, rb, t) + y)

    @pl.when(s == last)
    def _():
        for b in range(bt):
            full = st_ref[b].T
            for h in range(N_HEADS):
                gla_ref[b, h] = full[h * HEAD_K:(h + 1) * HEAD_K, h * HEAD_V:(h + 1) * HEAD_V]


def _ffn_kernel(x_ref, st0_ref, w_up_ref, conv_w_ref, w_down_ref, norm_ref, fnorm_ref,
                y_ref, st_ref, xn_ref, hid_ref, *, bt, t, final):
    m = bt * t
    s = pl.program_id(1)

    @pl.when(s == 0)
    def _():
        st_ref[...] = st0_ref[...]

    x = x_ref[...].reshape(m, D_MODEL)
    xn_ref[...] = (x * _rms_scale(x) * norm_ref[...]).astype(BF16)

    for j in range(D_FF // FFN_COL_BLOCK):
        halves = []
        for base in (0, D_FF):
            cols = slice(base + j * FFN_COL_BLOCK, base + (j + 1) * FFN_COL_BLOCK)
            u = jnp.dot(xn_ref[...], w_up_ref[:, cols], preferred_element_type=F32)
            w = conv_w_ref[:, cols]
            parts = []
            for b in range(bt):
                ub = u[b * t:(b + 1) * t]
                parts.append(_causal_taps(ub, st_ref[b, :, cols], w))
                st_ref[b, :, cols] = ub[t - 2:t]
            halves.append(parts[0] if bt == 1 else jnp.concatenate(parts, axis=0))
        a, g = halves
        hid_ref[:, j * FFN_COL_BLOCK:(j + 1) * FFN_COL_BLOCK] = (
            g * jax.nn.sigmoid(g) * a).astype(BF16)

    y = x_ref[...].reshape(m, D_MODEL) + jnp.dot(hid_ref[...], w_down_ref[...],
                                                  preferred_element_type=F32)
    if final:
        y = y * _rms_scale(y) * fnorm_ref[...]
    y_ref[...] = y.reshape(bt, t, D_MODEL)


def _const_spec(shape):
    return pl.BlockSpec(shape, lambda b, s: (0,) * len(shape), pipeline_mode=pl.Buffered(1))


def _block_sizes(batch, length, time_block):
    t = min(length, time_block)
    bt = max(1, min(batch, 256 // t))
    assert length % t == 0 and batch % bt == 0 and t % CHUNK == 0
    return bt, t


def _mixer_call(x, conv0, gla0, w_in, w_alpha, b_alpha, conv_w, g_conv, g_gla, w_out, norm):
    batch, length, _ = x.shape
    bt, t = _block_sizes(batch, length, MIXER_TIME_BLOCK)
    m = bt * t
    rb = min(m, MIXER_ROW_BLOCK)
    assert m % rb == 0 and (t % rb == 0 or rb % t == 0)
    x_spec = pl.BlockSpec((bt, t, D_MODEL), lambda b, s: (b, s, 0))
    conv_spec = pl.BlockSpec((bt, 2, D_CONV), lambda b, s: (b, 0, 0))
    gla_spec = pl.BlockSpec((bt, N_HEADS, HEAD_K, HEAD_V), lambda b, s: (b, 0, 0, 0))
    return pl.pallas_call(
        functools.partial(_mixer_kernel, bt=bt, t=t, rb=rb),
        grid=(batch // bt, length // t),
        in_specs=[x_spec, conv_spec, gla_spec,
                  _const_spec(w_in.shape), _const_spec(w_alpha.shape), _const_spec(b_alpha.shape),
                  _const_spec(conv_w.shape), _const_spec(g_conv.shape), _const_spec(g_gla.shape),
                  _const_spec(w_out.shape), _const_spec(norm.shape)],
        out_specs=[x_spec, conv_spec, gla_spec],
        out_shape=[jax.ShapeDtypeStruct(x.shape, F32),
                   jax.ShapeDtypeStruct(conv0.shape, F32),
                   jax.ShapeDtypeStruct(gla0.shape, F32)],
        scratch_shapes=[
            pltpu.VMEM((2, rb, D_MODEL), BF16),
            pltpu.VMEM((2, rb, D_IN_PAD), F32),
            pltpu.VMEM((2, rb, D_QK), BF16),
            pltpu.VMEM((2, rb, D_QK), BF16),
            pltpu.VMEM((2, rb, D_QK), BF16),
            pltpu.VMEM((2, rb, D_GLA), BF16),
            pltpu.VMEM((2, rb, D_GLA), F32),
            pltpu.VMEM((2, rb, D_MODEL), BF16),
            pltpu.VMEM((bt, D_GLA, D_QK), F32),
        ],
        compiler_params=pltpu.CompilerParams(
            dimension_semantics=("arbitrary", "arbitrary"),
            vmem_limit_bytes=VMEM_LIMIT_BYTES),
        name="mixer",
    )(x, conv0, gla0, w_in, w_alpha, b_alpha, conv_w, g_conv, g_gla, w_out, norm)


def _ffn_call(x, st0, w_up, conv_w, w_down, norm, fnorm, final):
    batch, length, _ = x.shape
    bt, t = _block_sizes(batch, length, FFN_TIME_BLOCK)
    m = bt * t
    x_spec = pl.BlockSpec((bt, t, D_MODEL), lambda b, s: (b, s, 0))
    st_spec = pl.BlockSpec((bt, 2, 2 * D_FF), lambda b, s: (b, 0, 0))
    return pl.pallas_call(
        functools.partial(_ffn_kernel, bt=bt, t=t, final=final),
        grid=(batch // bt, length // t),
        in_specs=[x_spec, st_spec, _const_spec(w_up.shape), _const_spec(conv_w.shape),
                  _const_spec(w_down.shape), _const_spec(norm.shape), _const_spec(fnorm.shape)],
        out_specs=[x_spec, st_spec],
        out_shape=[jax.ShapeDtypeStruct(x.shape, F32), jax.ShapeDtypeStruct(st0.shape, F32)],
        scratch_shapes=[pltpu.VMEM((m, D_MODEL), BF16),
                        pltpu.VMEM((m, D_FF), BF16)],
        compiler_params=pltpu.CompilerParams(
            dimension_semantics=("arbitrary", "arbitrary"),
            vmem_limit_bytes=VMEM_LIMIT_BYTES),
        name="ffn",
    )(x, st0, w_up, conv_w, w_down, norm, fnorm)


def _trunk(x, conv_s, gla_s, ffn_s, p):
    depth = conv_s.shape[0]
    new_c, new_g, new_f = [], [], []
    for l in range(depth):
        x, c, g = _mixer_call(x, conv_s[l], gla_s[l], p["w_in"][l], p["w_alpha"][l], p["b_alpha"][l],
                              p["conv_w"][l], p["g_conv"][l], p["g_gla"][l], p["w_out"][l],
                              p["norm1"][l])
        x, f = _ffn_call(x, ffn_s[l], p["w_up"][l], p["ffn_conv_w"][l], p["w_down"][l],
                         p["norm2"][l], p["final_norm"], final=(l == depth - 1))
        new_c.append(c)
        new_g.append(g)
        new_f.append(f)
    return x, jnp.stack(new_c), jnp.stack(new_g), jnp.stack(new_f)


def kernel(x_prompt, x_sample, state_conv, state_gla, state_ffn_conv, w_in, w_alpha, b_alpha, conv_w, g_conv, g_gla, w_out, norm1, norm2, w_up, ffn_conv_w, w_down, final_norm):
    depth = w_in.shape[0]
    gate_cols = w_in[:, :, C_R:C_R + GATE_RANK]
    w_in_r = jnp.concatenate(
        [w_in[:, :, :C_R], w_in[:, :, C_R + GATE_RANK:],
         jnp.pad(gate_cols, ((0, 0), (0, 0), (0, LANES - GATE_RANK)))], axis=-1).astype(BF16)
    w_alpha_p = jnp.pad(w_alpha, ((0, 0), (0, LANES - GATE_RANK), (0, 0))).astype(BF16)
    p = {
        "w_in": w_in_r, "w_alpha": w_alpha_p,
        "b_alpha": b_alpha.reshape(depth, 1, D_QK),
        "conv_w": conv_w, "g_conv": g_conv.reshape(depth, 1, D_CONV),
        "g_gla": g_gla.reshape(depth, 1, HEAD_V),
        "w_out": w_out.astype(BF16),
        "norm1": norm1.reshape(depth, 1, D_MODEL), "norm2": norm2.reshape(depth, 1, D_MODEL),
        "w_up": w_up.astype(BF16), "ffn_conv_w": ffn_conv_w, "w_down": w_down.astype(BF16),
        "final_norm": final_norm.reshape(1, D_MODEL),
    }
    bp = x_prompt.shape[0]
    dt = x_prompt.dtype
    zc = jnp.zeros((depth, bp) + state_conv.shape[2:], dt)
    zg = jnp.zeros((depth, bp) + state_gla.shape[2:], dt)
    zf = jnp.zeros((depth, bp) + state_ffn_conv.shape[2:], dt)
    y_prompt, pc, pg, pf = _trunk(x_prompt, zc, zg, zf, p)
    y_sample, sc, sg, sf = _trunk(x_sample, state_conv, state_gla, state_ffn_conv, p)
    return (y_prompt, y_sample, pc, pg, pf, sc, sg, sf)
```

```python
import functools

import jax
import jax.numpy as jnp
from jax import lax
from jax.experimental import pallas as pl
from jax.experimental.pallas import tpu as pltpu

F32 = jnp.float32
BF16 = jnp.bfloat16

D_MODEL = 1024
D_CONV = 512
N_HEADS = 4
HEAD_K = 64
HEAD_V = 128
D_QK = N_HEADS * HEAD_K
D_GLA = N_HEADS * HEAD_V
GATE_RANK = 16
GATE_TAU = 16.0
D_FF = 2816
CHUNK = 64
EPS = 1e-6

LANES = 128
SUBLANES = 8
C_BG, C_CG, C_H = 0, D_CONV, 2 * D_CONV
C_Q = 3 * D_CONV
C_K = C_Q + D_QK
C_V = C_K + D_QK
C_R = C_V + D_GLA
C_GR = C_R + D_GLA
D_IN_PAD = C_GR + LANES

FFN_COL_BLOCK = 256
FFN_TIME_BLOCK = 1024
MIXER_TIME_BLOCK = 1024
MIXER_ROW_BLOCK = 256
MIXER_PROJ_COLS = 512
VMEM_LIMIT_BYTES = 56 * 1024 * 1024


def _rms_scale(v):
    return lax.rsqrt(jnp.mean(v * v, axis=-1, keepdims=True) + EPS)


def _log_sigmoid(v):
    return jnp.minimum(v, 0.0) - jnp.log1p(jnp.exp(-jnp.abs(v)))


def _causal_taps(u, prev, w):
    t = u.shape[0]
    row = lax.broadcasted_iota(jnp.int32, (t, 1), 0)
    s1 = jnp.where(row == 0, prev[1:2], pltpu.roll(u, 1, 0))
    s2 = jnp.where(row == 0, prev[0:1], jnp.where(row == 1, prev[1:2], pltpu.roll(u, 2, 0)))
    return s2 * w[0:1] + s1 * w[1:2] + u * w[2:3]


def _causal_taps_interleaved(u, prev, w):
    t = u.shape[0]
    sub = lax.broadcasted_iota(jnp.int32, (SUBLANES, 1), 0)
    before1 = jnp.where(sub == 0, prev[1:2], pltpu.roll(u[t - SUBLANES:t], 1, 0))
    before2 = jnp.where(sub == 0, prev[0:1],
                        pltpu.roll(u[t - 2 * SUBLANES:t - SUBLANES], 1, 0))
    s1 = jnp.concatenate([before1, u[:t - SUBLANES]], axis=0)
    s2 = jnp.concatenate([before2, before1, u[:t - 2 * SUBLANES]], axis=0)
    return s2 * w[0:1] + s1 * w[1:2] + u * w[2:3]


def _half_lane_mask(rows, head):
    lane = lax.broadcasted_iota(jnp.int32, (rows, LANES), 1)
    return (lane // HEAD_K) == (head % 2)


def _block_rows(ref, r0, n, t):
    if t >= n:
        return ref[r0 // t, r0 % t:r0 % t + n, :]
    return ref[r0 // t:(r0 + n) // t].reshape(n, ref.shape[-1])


def _store_block_rows(ref, r0, n, t, val):
    if t >= n:
        ref[r0 // t, r0 % t:r0 % t + n, :] = val
    else:
        ref[r0 // t:(r0 + n) // t] = val.reshape(n // t, t, ref.shape[-1])


def _mixer_kernel(x_ref, conv0_ref, gla0_ref, w_in_ref, w_alpha_ref, b_alpha_ref, conv_w_ref,
                  g_conv_ref, g_gla_ref, w_out_ref, norm_ref,
                  y_ref, conv_ref, gla_ref,
                  xn_ref, z_ref, qt_ref, kt_ref, kd_ref, v_ref, o_ref, ycat_ref, st_ref,
                  *, bt, t, rb):
    m = bt * t
    ncs = rb // CHUNK
    s = pl.program_id(1)
    last = pl.num_programs(1) - 1

    @pl.when(s == 0)
    def _():
        conv_ref[...] = conv0_ref[...]
        kv_mask = (lax.broadcasted_iota(jnp.int32, (D_QK, D_GLA), 0) // HEAD_K
                   == lax.broadcasted_iota(jnp.int32, (D_QK, D_GLA), 1) // HEAD_V)
        for b in range(bt):
            s0 = gla0_ref[b].reshape(D_QK, HEAD_V)
            s0 = jnp.where(kv_mask, jnp.concatenate([s0] * N_HEADS, axis=1), 0.0)
            st_ref[b] = s0.T

    causal = (lax.broadcasted_iota(jnp.int32, (CHUNK, D_QK), 1) % CHUNK
              <= lax.broadcasted_iota(jnp.int32, (CHUNK, D_QK), 0))
    zero_kk = jnp.zeros((CHUNK, LANES), BF16)
    zero_vv = jnp.zeros((CHUNK, HEAD_V), BF16)
    zero_st = jnp.zeros((HEAD_V, LANES), BF16)

    def projection_pieces(r0):
        slot = (r0 // rb) % 2
        x = _block_rows(x_ref, r0, rb, t)
        xn_ref[slot] = (x * _rms_scale(x) * norm_ref[...]).astype(BF16)

        def piece(c0):
            c1 = min(c0 + MIXER_PROJ_COLS, D_IN_PAD)
            z_ref[slot, :, c0:c1] = jnp.dot(xn_ref[slot], w_in_ref[:, c0:c1],
                                            preferred_element_type=F32)
        return [functools.partial(piece, c0) for c0 in range(0, D_IN_PAD, MIXER_PROJ_COLS)]

    def run(pieces, n):
        for _ in range(min(n, len(pieces))):
            pieces.pop(0)()

    first = projection_pieces(0)
    run(first, len(first))
    for r0 in range(0, m, rb):
        slot = (r0 // rb) % 2
        z = z_ref.at[slot]
        qt_s, kt_s, kd_s, v_s = qt_ref.at[slot], kt_ref.at[slot], kd_ref.at[slot], v_ref.at[slot]
        o_s, ycat_s = o_ref.at[slot], ycat_ref.at[slot]
        ahead = projection_pieces(r0 + rb) if r0 + rb < m else []

        gate = jnp.dot(z[:, C_GR:C_GR + LANES].astype(BF16), w_alpha_ref[...],
                       preferred_element_type=F32) + b_alpha_ref[...]
        run(ahead, 2)

        piece = min(t, rb)
        for p0 in range(0, rb, piece):
            b = (r0 + p0) // t
            prow = slice(p0, p0 + piece)
            ch = z[prow, C_CG:C_CG + D_CONV] * z[prow, C_H:C_H + D_CONV]
            cv = _causal_taps(ch, conv_ref[b], conv_w_ref[...])
            conv_ref[b] = ch[piece - 2:piece]
            yc = z[prow, C_BG:C_BG + D_CONV] * cv
            ycat_s[prow, 0:D_CONV] = (yc * _rms_scale(yc) * g_conv_ref[...]).astype(BF16)

        bcum = _log_sigmoid(gate) * (1.0 / GATE_TAU)
        pos = lax.broadcasted_iota(jnp.int32, (rb, 1), 0) % CHUNK
        shift = 1
        while shift < CHUNK:
            bcum = bcum + jnp.where(pos >= shift, pltpu.roll(bcum, shift, 0), 0.0)
            shift *= 2
        b3 = bcum.reshape(ncs, CHUNK, D_QK)
        b_last = b3[:, CHUNK - 1:CHUNK, :]
        q = z[:, C_Q:C_Q + D_QK] * (HEAD_K ** -0.5)
        k = z[:, C_K:C_K + D_QK]
        qt_s[...] = (q * jnp.exp(bcum)).astype(BF16)
        kt_s[...] = (k * jnp.exp(-bcum)).astype(BF16)
        kd_s[...] = (k.reshape(ncs, CHUNK, D_QK)
                     * jnp.exp(b_last - b3)).reshape(rb, D_QK).astype(BF16)
        dec = jnp.exp(b_last)
        v_s[...] = z[:, C_V:C_V + D_GLA].astype(BF16)

        for c in range(ncs):
            b = (r0 + c * CHUNK) // t
            crow = slice(c * CHUNK, (c + 1) * CHUNK)
            qt = qt_s[crow, :]
            kt = kt_s[crow, :]
            kd = kd_s[crow, :]
            v = v_s[crow, :]
            k_bd = jnp.concatenate([
                jnp.concatenate([
                    jnp.where(_half_lane_mask(CHUNK, h), kt[:, g * LANES:(g + 1) * LANES], 0.0)
                    if g == h // 2 else zero_kk for g in range(D_QK // LANES)], axis=1)
                for h in range(N_HEADS)], axis=0)
            att = lax.dot_general(qt, k_bd, (((1,), (1,)), ((), ())), preferred_element_type=F32)
            ds_t = lax.dot_general(v, kd, (((0,), (0,)), ((), ())), preferred_element_type=F32)
            run(ahead, 1)
            att = jnp.where(causal, att, 0.0)
            v_bd = jnp.concatenate([
                jnp.concatenate([v[:, h * HEAD_V:(h + 1) * HEAD_V] if g == h else zero_vv
                                 for g in range(N_HEADS)], axis=1)
                for h in range(N_HEADS)], axis=0)
            st_blocks = [st_ref[b, h * HEAD_V:(h + 1) * HEAD_V,
                                (h // 2) * LANES:(h // 2 + 1) * LANES] for h in range(N_HEADS)]
            st_bf = jnp.concatenate([
                jnp.concatenate([st_blocks[h].astype(BF16) if g == h // 2 else zero_st
                                 for g in range(D_QK // LANES)], axis=1)
                for h in range(N_HEADS)], axis=0)
            o = jnp.dot(att.astype(BF16), v_bd, preferred_element_type=F32)
            o = o + lax.dot_general(qt, st_bf, (((1,), (1,)), ((), ())),
                                    preferred_element_type=F32)
            o_s[crow, :] = o
            for h in range(N_HEADS):
                g = h // 2
                upd = jnp.where(_half_lane_mask(HEAD_V, h),
                                ds_t[h * HEAD_V:(h + 1) * HEAD_V, g * LANES:(g + 1) * LANES], 0.0)
                st_ref[b, h * HEAD_V:(h + 1) * HEAD_V, g * LANES:(g + 1) * LANES] = (
                    st_blocks[h] * dec[c, :, g * LANES:(g + 1) * LANES] + upd)

        run(ahead, len(ahead))
        for h in range(N_HEADS):
            o = o_s[:, h * HEAD_V:(h + 1) * HEAD_V]
            r = z[:, C_R + h * HEAD_V:C_R + (h + 1) * HEAD_V]
            o = o * _rms_scale(o) * g_gla_ref[...]
            ycat_s[:, D_CONV + h * HEAD_V:D_CONV + (h + 1) * HEAD_V] = (
                o * (r * jax.nn.sigmoid(r))).astype(BF16)
        y = jnp.dot(ycat_s[...], w_out_ref[...], preferred_element_type=F32)
        _store_block_rows(y_ref, r0, rb, t, _block_rows(x_ref, r0, rb, t) + y)

    @pl.when(s == last)
    def _():
        for b in range(bt):
            full = st_ref[b].T
            for h in range(N_HEADS):
                gla_ref[b, h] = full[h * HEAD_K:(h + 1) * HEAD_K, h * HEAD_V:(h + 1) * HEAD_V]


def _ffn_kernel(x_ref, st0_ref, w_up_ref, conv_w_ref, w_down_ref, norm_ref, fnorm_ref,
                y_ref, st_ref, perm_ref, xn_ref, hid_ref, *, bt, t, final):
    m = bt * t
    s = pl.program_id(1)
    g = t // SUBLANES
    n_slabs = D_MODEL // LANES

    @pl.when(s == 0)
    def _():
        st_ref[...] = st0_ref[...]

    for b in range(bt):
        x = x_ref[b]
        xn = x * _rms_scale(x) * norm_ref[...]
        for a in range(SUBLANES):
            for l in range(n_slabs):
                perm_ref[l, pl.ds(b * t + a, g, stride=SUBLANES), :] = (
                    xn[a * g:(a + 1) * g, l * LANES:(l + 1) * LANES])
    xn_ref[...] = jnp.concatenate([perm_ref[l] for l in range(n_slabs)], axis=1).astype(BF16)

    for j in range(D_FF // FFN_COL_BLOCK):
        halves = []
        for base in (0, D_FF):
            cols = slice(base + j * FFN_COL_BLOCK, base + (j + 1) * FFN_COL_BLOCK)
            u = jnp.dot(xn_ref[...], w_up_ref[:, cols], preferred_element_type=F32)
            w = conv_w_ref[:, cols]
            parts = []
            for b in range(bt):
                ub = u[b * t:(b + 1) * t]
                parts.append(_causal_taps_interleaved(ub, st_ref[b, :, cols], w))
                st_ref[b, 0:1, cols] = ub[t - SUBLANES - 1:t - SUBLANES]
                st_ref[b, 1:2, cols] = ub[t - 1:t]
            halves.append(parts[0] if bt == 1 else jnp.concatenate(parts, axis=0))
        a, gate = halves
        hid_ref[:, j * FFN_COL_BLOCK:(j + 1) * FFN_COL_BLOCK] = (
            gate * jax.nn.sigmoid(gate) * a).astype(BF16)

    y = jnp.dot(hid_ref[...], w_down_ref[...], preferred_element_type=F32)
    for l in range(n_slabs):
        perm_ref[l] = y[:, l * LANES:(l + 1) * LANES]
    for b in range(bt):
        y = jnp.concatenate([
            jnp.concatenate([perm_ref[l, pl.ds(b * t + a, g, stride=SUBLANES), :]
                             for l in range(n_slabs)], axis=1)
            for a in range(SUBLANES)], axis=0)
        y = x_ref[b] + y
        if final:
            y = y * _rms_scale(y) * fnorm_ref[...]
        y_ref[b] = y


def _const_spec(shape):
    return pl.BlockSpec(shape, lambda b, s: (0,) * len(shape), pipeline_mode=pl.Buffered(1))


def _block_sizes(batch, length, time_block):
    t = min(length, time_block)
    bt = max(1, min(batch, 256 // t))
    assert length % t == 0 and batch % bt == 0 and t % CHUNK == 0
    return bt, t


def _mixer_call(x, conv0, gla0, w_in, w_alpha, b_alpha, conv_w, g_conv, g_gla, w_out, norm):
    batch, length, _ = x.shape
    bt, t = _block_sizes(batch, length, MIXER_TIME_BLOCK)
    m = bt * t
    rb = min(m, MIXER_ROW_BLOCK)
    assert m % rb == 0 and (t % rb == 0 or rb % t == 0)
    x_spec = pl.BlockSpec((bt, t, D_MODEL), lambda b, s: (b, s, 0))
    conv_spec = pl.BlockSpec((bt, 2, D_CONV), lambda b, s: (b, 0, 0))
    gla_spec = pl.BlockSpec((bt, N_HEADS, HEAD_K, HEAD_V), lambda b, s: (b, 0, 0, 0))
    return pl.pallas_call(
        functools.partial(_mixer_kernel, bt=bt, t=t, rb=rb),
        grid=(batch // bt, length // t),
        in_specs=[x_spec, conv_spec, gla_spec,
                  _const_spec(w_in.shape), _const_spec(w_alpha.shape), _const_spec(b_alpha.shape),
                  _const_spec(conv_w.shape), _const_spec(g_conv.shape), _const_spec(g_gla.shape),
                  _const_spec(w_out.shape), _const_spec(norm.shape)],
        out_specs=[x_spec, conv_spec, gla_spec],
        out_shape=[jax.ShapeDtypeStruct(x.shape, F32),
                   jax.ShapeDtypeStruct(conv0.shape, F32),
                   jax.ShapeDtypeStruct(gla0.shape, F32)],
        scratch_shapes=[
            pltpu.VMEM((2, rb, D_MODEL), BF16),
            pltpu.VMEM((2, rb, D_IN_PAD), F32),
            pltpu.VMEM((2, rb, D_QK), BF16),
            pltpu.VMEM((2, rb, D_QK), BF16),
            pltpu.VMEM((2, rb, D_QK), BF16),
            pltpu.VMEM((2, rb, D_GLA), BF16),
            pltpu.VMEM((2, rb, D_GLA), F32),
            pltpu.VMEM((2, rb, D_MODEL), BF16),
            pltpu.VMEM((bt, D_GLA, D_QK), F32),
        ],
        compiler_params=pltpu.CompilerParams(
            dimension_semantics=("arbitrary", "arbitrary"),
            vmem_limit_bytes=VMEM_LIMIT_BYTES),
        name="mixer",
    )(x, conv0, gla0, w_in, w_alpha, b_alpha, conv_w, g_conv, g_gla, w_out, norm)


def _ffn_call(x, st0, w_up, conv_w, w_down, norm, fnorm, final):
    batch, length, _ = x.shape
    bt, t = _block_sizes(batch, length, FFN_TIME_BLOCK)
    m = bt * t
    x_spec = pl.BlockSpec((bt, t, D_MODEL), lambda b, s: (b, s, 0))
    st_spec = pl.BlockSpec((bt, 2, 2 * D_FF), lambda b, s: (b, 0, 0))
    return pl.pallas_call(
        functools.partial(_ffn_kernel, bt=bt, t=t, final=final),
        grid=(batch // bt, length // t),
        in_specs=[x_spec, st_spec, _const_spec(w_up.shape), _const_spec(conv_w.shape),
                  _const_spec(w_down.shape), _const_spec(norm.shape), _const_spec(fnorm.shape)],
        out_specs=[x_spec, st_spec],
        out_shape=[jax.ShapeDtypeStruct(x.shape, F32), jax.ShapeDtypeStruct(st0.shape, F32)],
        scratch_shapes=[pltpu.VMEM((D_MODEL // LANES, m, LANES), F32),
                        pltpu.VMEM((m, D_MODEL), BF16),
                        pltpu.VMEM((m, D_FF), BF16)],
        compiler_params=pltpu.CompilerParams(
            dimension_semantics=("arbitrary", "arbitrary"),
            vmem_limit_bytes=VMEM_LIMIT_BYTES),
        name="ffn",
    )(x, st0, w_up, conv_w, w_down, norm, fnorm)


def _trunk(x, conv_s, gla_s, ffn_s, p):
    depth = conv_s.shape[0]
    new_c, new_g, new_f = [], [], []
    for l in range(depth):
        x, c, g = _mixer_call(x, conv_s[l], gla_s[l], p["w_in"][l], p["w_alpha"][l], p["b_alpha"][l],
                              p["conv_w"][l], p["g_conv"][l], p["g_gla"][l], p["w_out"][l],
                              p["norm1"][l])
        x, f = _ffn_call(x, ffn_s[l], p["w_up"][l], p["ffn_conv_w"][l], p["w_down"][l],
                         p["norm2"][l], p["final_norm"], final=(l == depth - 1))
        new_c.append(c)
        new_g.append(g)
        new_f.append(f)
    return x, jnp.stack(new_c), jnp.stack(new_g), jnp.stack(new_f)


def kernel(x_prompt, x_sample, state_conv, state_gla, state_ffn_conv, w_in, w_alpha, b_alpha, conv_w, g_conv, g_gla, w_out, norm1, norm2, w_up, ffn_conv_w, w_down, final_norm):
    depth = w_in.shape[0]
    gate_cols = w_in[:, :, C_R:C_R + GATE_RANK]
    w_in_r = jnp.concatenate(
        [w_in[:, :, :C_R], w_in[:, :, C_R + GATE_RANK:],
         jnp.pad(gate_cols, ((0, 0), (0, 0), (0, LANES - GATE_RANK)))], axis=-1).astype(BF16)
    w_alpha_p = jnp.pad(w_alpha, ((0, 0), (0, LANES - GATE_RANK), (0, 0))).astype(BF16)
    p = {
        "w_in": w_in_r, "w_alpha": w_alpha_p,
        "b_alpha": b_alpha.reshape(depth, 1, D_QK),
        "conv_w": conv_w, "g_conv": g_conv.reshape(depth, 1, D_CONV),
        "g_gla": g_gla.reshape(depth, 1, HEAD_V),
        "w_out": w_out.astype(BF16),
        "norm1": norm1.reshape(depth, 1, D_MODEL), "norm2": norm2.reshape(depth, 1, D_MODEL),
        "w_up": w_up.astype(BF16), "ffn_conv_w": ffn_conv_w, "w_down": w_down.astype(BF16),
        "final_norm": final_norm.reshape(1, D_MODEL),
    }
    bp = x_prompt.shape[0]
    dt = x_prompt.dtype
    zc = jnp.zeros((depth, bp) + state_conv.shape[2:], dt)
    zg = jnp.zeros((depth, bp) + state_gla.shape[2:], dt)
    zf = jnp.zeros((depth, bp) + state_ffn_conv.shape[2:], dt)
    y_prompt, pc, pg, pf = _trunk(x_prompt, zc, zg, zf, p)
    y_sample, sc, sg, sf = _trunk(x_sample, state_conv, state_gla, state_ffn_conv, p)
    return (y_prompt, y_sample, pc, pg, pf, sc, sg, sf)
```

```python
import functools

import jax
import jax.numpy as jnp
from jax import lax
from jax.experimental import pallas as pl
from jax.experimental.pallas import tpu as pltpu

F32 = jnp.float32
BF16 = jnp.bfloat16

D_MODEL = 1024
D_CONV = 512
N_HEADS = 4
HEAD_K = 64
HEAD_V = 128
D_QK = N_HEADS * HEAD_K
D_GLA = N_HEADS * HEAD_V
GATE_RANK = 16
GATE_TAU = 16.0
D_FF = 2816
CHUNK = 64
EPS = 1e-6

LANES = 128
SUBLANES = 8
C_BG, C_CG, C_H = 0, D_CONV, 2 * D_CONV
C_Q = 3 * D_CONV
C_K = C_Q + D_QK
C_V = C_K + D_QK
C_R = C_V + D_GLA
C_GR = C_R + D_GLA
D_IN_PAD = C_GR + LANES

FFN_COL_BLOCK = 256
FFN_TIME_BLOCK = 1024
MIXER_TIME_BLOCK = 2048
MIXER_ROW_BLOCK = 256
MIXER_PROJ_COLS = 512
VMEM_LIMIT_BYTES = 56 * 1024 * 1024


def _rms_scale(v):
    return lax.rsqrt(jnp.mean(v * v, axis=-1, keepdims=True) + EPS)


def _log_sigmoid(v):
    return jnp.minimum(v, 0.0) - jnp.log1p(jnp.exp(-jnp.abs(v)))


def _causal_taps(u, prev, w):
    t = u.shape[0]
    row = lax.broadcasted_iota(jnp.int32, (t, 1), 0)
    s1 = jnp.where(row == 0, prev[1:2], pltpu.roll(u, 1, 0))
    s2 = jnp.where(row == 0, prev[0:1], jnp.where(row == 1, prev[1:2], pltpu.roll(u, 2, 0)))
    return s2 * w[0:1] + s1 * w[1:2] + u * w[2:3]


def _causal_taps_interleaved(u, prev, w):
    t = u.shape[0]
    sub = lax.broadcasted_iota(jnp.int32, (SUBLANES, 1), 0)
    before1 = jnp.where(sub == 0, prev[1:2], pltpu.roll(u[t - SUBLANES:t], 1, 0))
    before2 = jnp.where(sub == 0, prev[0:1],
                        pltpu.roll(u[t - 2 * SUBLANES:t - SUBLANES], 1, 0))
    s1 = jnp.concatenate([before1, u[:t - SUBLANES]], axis=0)
    s2 = jnp.concatenate([before2, before1, u[:t - 2 * SUBLANES]], axis=0)
    return s2 * w[0:1] + s1 * w[1:2] + u * w[2:3]


def _half_lane_mask(rows, head):
    lane = lax.broadcasted_iota(jnp.int32, (rows, LANES), 1)
    return (lane // HEAD_K) == (head % 2)


def _block_rows(ref, r0, n, t):
    if t >= n:
        return ref[r0 // t, r0 % t:r0 % t + n, :]
    return ref[r0 // t:(r0 + n) // t].reshape(n, ref.shape[-1])


def _store_block_rows(ref, r0, n, t, val):
    if t >= n:
        ref[r0 // t, r0 % t:r0 % t + n, :] = val
    else:
        ref[r0 // t:(r0 + n) // t] = val.reshape(n // t, t, ref.shape[-1])


def _mixer_kernel(x_ref, conv0_ref, gla0_ref, w_in_ref, w_alpha_ref, b_alpha_ref, conv_w_ref,
                  g_conv_ref, g_gla_ref, w_out_ref, norm_ref,
                  y_ref, conv_ref, gla_ref,
                  xn_ref, z_ref, qt_ref, kt_ref, kd_ref, v_ref, o_ref, ycat_ref, st_ref,
                  *, bt, t, rb):
    m = bt * t
    ncs = rb // CHUNK
    s = pl.program_id(1)
    last = pl.num_programs(1) - 1

    @pl.when(s == 0)
    def _():
        conv_ref[...] = conv0_ref[...]
        kv_mask = (lax.broadcasted_iota(jnp.int32, (D_QK, D_GLA), 0) // HEAD_K
                   == lax.broadcasted_iota(jnp.int32, (D_QK, D_GLA), 1) // HEAD_V)
        for b in range(bt):
            s0 = gla0_ref[b].reshape(D_QK, HEAD_V)
            s0 = jnp.where(kv_mask, jnp.concatenate([s0] * N_HEADS, axis=1), 0.0)
            st_ref[b] = s0.T

    causal = (lax.broadcasted_iota(jnp.int32, (CHUNK, D_QK), 1) % CHUNK
              <= lax.broadcasted_iota(jnp.int32, (CHUNK, D_QK), 0))
    zero_kk = jnp.zeros((CHUNK, LANES), BF16)
    zero_vv = jnp.zeros((CHUNK, HEAD_V), BF16)
    zero_st = jnp.zeros((HEAD_V, LANES), BF16)

    def projection_pieces(r0):
        slot = (r0 // rb) % 2
        x = _block_rows(x_ref, r0, rb, t)
        xn_ref[slot] = (x * _rms_scale(x) * norm_ref[...]).astype(BF16)

        def piece(c0):
            c1 = min(c0 + MIXER_PROJ_COLS, D_IN_PAD)
            z_ref[slot, :, c0:c1] = jnp.dot(xn_ref[slot], w_in_ref[:, c0:c1],
                                            preferred_element_type=F32)
        return [functools.partial(piece, c0) for c0 in range(0, D_IN_PAD, MIXER_PROJ_COLS)]

    def run(pieces, n):
        for _ in range(min(n, len(pieces))):
            pieces.pop(0)()

    first = projection_pieces(0)
    run(first, len(first))
    pending_out = None
    for r0 in range(0, m, rb):
        slot = (r0 // rb) % 2
        z = z_ref.at[slot]
        qt_s, kt_s, kd_s, v_s = qt_ref.at[slot], kt_ref.at[slot], kd_ref.at[slot], v_ref.at[slot]
        o_s, ycat_s = o_ref.at[slot], ycat_ref.at[slot]
        ahead = projection_pieces(r0 + rb) if r0 + rb < m else []

        gate = jnp.dot(z[:, C_GR:C_GR + LANES].astype(BF16), w_alpha_ref[...],
                       preferred_element_type=F32) + b_alpha_ref[...]
        if pending_out is not None:
            pending_out()
        run(ahead, 2)

        bcum = _log_sigmoid(gate) * (1.0 / GATE_TAU)
        pos = lax.broadcasted_iota(jnp.int32, (rb, 1), 0) % CHUNK
        shift = 1
        while shift < CHUNK:
            bcum = bcum + jnp.where(pos >= shift, pltpu.roll(bcum, shift, 0), 0.0)
            shift *= 2
        b3 = bcum.reshape(ncs, CHUNK, D_QK)
        b_last = b3[:, CHUNK - 1:CHUNK, :]
        q = z[:, C_Q:C_Q + D_QK] * (HEAD_K ** -0.5)
        k = z[:, C_K:C_K + D_QK]
        qt_s[...] = (q * jnp.exp(bcum)).astype(BF16)
        kt_s[...] = (k * jnp.exp(-bcum)).astype(BF16)
        kd_s[...] = (k.reshape(ncs, CHUNK, D_QK)
                     * jnp.exp(b_last - b3)).reshape(rb, D_QK).astype(BF16)
        dec = jnp.exp(b_last)
        v_s[...] = z[:, C_V:C_V + D_GLA].astype(BF16)

        for c in range(ncs):
            b = (r0 + c * CHUNK) // t
            crow = slice(c * CHUNK, (c + 1) * CHUNK)
            qt = qt_s[crow, :]
            kt = kt_s[crow, :]
            kd = kd_s[crow, :]
            v = v_s[crow, :]
            k_bd = jnp.concatenate([
                jnp.concatenate([
                    jnp.where(_half_lane_mask(CHUNK, h), kt[:, g * LANES:(g + 1) * LANES], 0.0)
                    if g == h // 2 else zero_kk for g in range(D_QK // LANES)], axis=1)
                for h in range(N_HEADS)], axis=0)
            att = lax.dot_general(qt, k_bd, (((1,), (1,)), ((), ())), preferred_element_type=F32)
            ds_t = lax.dot_general(v, kd, (((0,), (0,)), ((), ())), preferred_element_type=F32)
            run(ahead, 1)
            att = jnp.where(causal, att, 0.0)
            v_bd = jnp.concatenate([
                jnp.concatenate([v[:, h * HEAD_V:(h + 1) * HEAD_V] if g == h else zero_vv
                                 for g in range(N_HEADS)], axis=1)
                for h in range(N_HEADS)], axis=0)
            st_blocks = [st_ref[b, h * HEAD_V:(h + 1) * HEAD_V,
                                (h // 2) * LANES:(h // 2 + 1) * LANES] for h in range(N_HEADS)]
            st_bf = jnp.concatenate([
                jnp.concatenate([st_blocks[h].astype(BF16) if g == h // 2 else zero_st
                                 for g in range(D_QK // LANES)], axis=1)
                for h in range(N_HEADS)], axis=0)
            o = jnp.dot(att.astype(BF16), v_bd, preferred_element_type=F32)
            o = o + lax.dot_general(qt, st_bf, (((1,), (1,)), ((), ())),
                                    preferred_element_type=F32)
            o_s[crow, :] = o
            for h in range(N_HEADS):
                g = h // 2
                upd = jnp.where(_half_lane_mask(HEAD_V, h),
                                ds_t[h * HEAD_V:(h + 1) * HEAD_V, g * LANES:(g + 1) * LANES], 0.0)
                st_ref[b, h * HEAD_V:(h + 1) * HEAD_V, g * LANES:(g + 1) * LANES] = (
                    st_blocks[h] * dec[c, :, g * LANES:(g + 1) * LANES] + upd)

        piece = min(t, rb)
        for p0 in range(0, rb, piece):
            b = (r0 + p0) // t
            prow = slice(p0, p0 + piece)
            ch = z[prow, C_CG:C_CG + D_CONV] * z[prow, C_H:C_H + D_CONV]
            cv = _causal_taps(ch, conv_ref[b], conv_w_ref[...])
            conv_ref[b] = ch[piece - 2:piece]
            yc = z[prow, C_BG:C_BG + D_CONV] * cv
            ycat_s[prow, 0:D_CONV] = (yc * _rms_scale(yc) * g_conv_ref[...]).astype(BF16)

        run(ahead, len(ahead))
        for h in range(N_HEADS):
            o = o_s[:, h * HEAD_V:(h + 1) * HEAD_V]
            r = z[:, C_R + h * HEAD_V:C_R + (h + 1) * HEAD_V]
            o = o * _rms_scale(o) * g_gla_ref[...]
            ycat_s[:, D_CONV + h * HEAD_V:D_CONV + (h + 1) * HEAD_V] = (
                o * (r * jax.nn.sigmoid(r))).astype(BF16)

        def project_out(r0=r0, ycat_s=ycat_s):
            y = jnp.dot(ycat_s[...], w_out_ref[...], preferred_element_type=F32)
            _store_block_rows(y_ref, r0, rb, t, _block_rows(x_ref, r0, rb, t) + y)
        pending_out = project_out
    pending_out()

    @pl.when(s == last)
    def _():
        for b in range(bt):
            full = st_ref[b].T
            for h in range(N_HEADS):
                gla_ref[b, h] = full[h * HEAD_K:(h + 1) * HEAD_K, h * HEAD_V:(h + 1) * HEAD_V]


def _ffn_kernel(x_ref, st0_ref, w_up_ref, conv_w_ref, w_down_ref, norm_ref, fnorm_ref,
                y_ref, st_ref, perm_ref, xn_ref, hid_ref, *, bt, t, final):
    m = bt * t
    s = pl.program_id(1)
    g = t // SUBLANES
    n_slabs = D_MODEL // LANES

    @pl.when(s == 0)
    def _():
        st_ref[...] = st0_ref[...]

    for b in range(bt):
        x = x_ref[b]
        xn = x * _rms_scale(x) * norm_ref[...]
        for a in range(SUBLANES):
            for l in range(n_slabs):
                perm_ref[l, pl.ds(b * t + a, g, stride=SUBLANES), :] = (
                    xn[a * g:(a + 1) * g, l * LANES:(l + 1) * LANES])
    xn_ref[...] = jnp.concatenate([perm_ref[l] for l in range(n_slabs)], axis=1).astype(BF16)

    for j in range(D_FF // FFN_COL_BLOCK):
        halves = []
        for base in (0, D_FF):
            cols = slice(base + j * FFN_COL_BLOCK, base + (j + 1) * FFN_COL_BLOCK)
            u = jnp.dot(xn_ref[...], w_up_ref[:, cols], preferred_element_type=F32)
            w = conv_w_ref[:, cols]
            parts = []
            for b in range(bt):
                ub = u[b * t:(b + 1) * t]
                parts.append(_causal_taps_interleaved(ub, st_ref[b, :, cols], w))
                st_ref[b, 0:1, cols] = ub[t - SUBLANES - 1:t - SUBLANES]
                st_ref[b, 1:2, cols] = ub[t - 1:t]
            halves.append(parts[0] if bt == 1 else jnp.concatenate(parts, axis=0))
        a, gate = halves
        hid_ref[:, j * FFN_COL_BLOCK:(j + 1) * FFN_COL_BLOCK] = (
            gate * jax.nn.sigmoid(gate) * a).astype(BF16)

    y = jnp.dot(hid_ref[...], w_down_ref[...], preferred_element_type=F32)
    for l in range(n_slabs):
        perm_ref[l] = y[:, l * LANES:(l + 1) * LANES]
    for b in range(bt):
        y = jnp.concatenate([
            jnp.concatenate([perm_ref[l, pl.ds(b * t + a, g, stride=SUBLANES), :]
                             for l in range(n_slabs)], axis=1)
            for a in range(SUBLANES)], axis=0)
        y = x_ref[b] + y
        if final:
            y = y * _rms_scale(y) * fnorm_ref[...]
        y_ref[b] = y


def _const_spec(shape):
    return pl.BlockSpec(shape, lambda b, s: (0,) * len(shape), pipeline_mode=pl.Buffered(1))


def _layer_spec(stacked, layer):
    rest = stacked.shape[1:]
    return pl.BlockSpec((None,) + rest, lambda b, s: (layer,) + (0,) * len(rest),
                        pipeline_mode=pl.Buffered(1))


def _state_spec(stacked, layer, bt):
    rest = stacked.shape[2:]
    return pl.BlockSpec((None, bt) + rest, lambda b, s: (layer, b) + (0,) * len(rest))


def _block_sizes(batch, length, time_block):
    t = min(length, time_block)
    bt = max(1, min(batch, 256 // t))
    assert length % t == 0 and batch % bt == 0 and t % CHUNK == 0
    return bt, t


def _mixer_call(x, layer, conv0, gla0, w_in, w_alpha, b_alpha, conv_w, g_conv, g_gla, w_out, norm):
    batch, length, _ = x.shape
    bt, t = _block_sizes(batch, length, MIXER_TIME_BLOCK)
    m = bt * t
    rb = min(m, MIXER_ROW_BLOCK)
    assert m % rb == 0 and (t % rb == 0 or rb % t == 0)
    x_spec = pl.BlockSpec((bt, t, D_MODEL), lambda b, s: (b, s, 0))
    conv_spec = pl.BlockSpec((bt, 2, D_CONV), lambda b, s: (b, 0, 0))
    gla_spec = pl.BlockSpec((bt, N_HEADS, HEAD_K, HEAD_V), lambda b, s: (b, 0, 0, 0))
    return pl.pallas_call(
        functools.partial(_mixer_kernel, bt=bt, t=t, rb=rb),
        grid=(batch // bt, length // t),
        in_specs=[x_spec, _state_spec(conv0, layer, bt), _state_spec(gla0, layer, bt)]
        + [_layer_spec(w, layer)
           for w in (w_in, w_alpha, b_alpha, conv_w, g_conv, g_gla, w_out, norm)],
        out_specs=[x_spec, conv_spec, gla_spec],
        out_shape=[jax.ShapeDtypeStruct(x.shape, F32),
                   jax.ShapeDtypeStruct(conv0.shape[1:], F32),
                   jax.ShapeDtypeStruct(gla0.shape[1:], F32)],
        scratch_shapes=[
            pltpu.VMEM((2, rb, D_MODEL), BF16),
            pltpu.VMEM((2, rb, D_IN_PAD), F32),
            pltpu.VMEM((2, rb, D_QK), BF16),
            pltpu.VMEM((2, rb, D_QK), BF16),
            pltpu.VMEM((2, rb, D_QK), BF16),
            pltpu.VMEM((2, rb, D_GLA), BF16),
            pltpu.VMEM((2, rb, D_GLA), F32),
            pltpu.VMEM((2, rb, D_MODEL), BF16),
            pltpu.VMEM((bt, D_GLA, D_QK), F32),
        ],
        compiler_params=pltpu.CompilerParams(
            dimension_semantics=("arbitrary", "arbitrary"),
            vmem_limit_bytes=VMEM_LIMIT_BYTES),
        name="mixer",
    )(x, conv0, gla0, w_in, w_alpha, b_alpha, conv_w, g_conv, g_gla, w_out, norm)


def _ffn_call(x, layer, st0, w_up, conv_w, w_down, norm, fnorm, final):
    batch, length, _ = x.shape
    bt, t = _block_sizes(batch, length, FFN_TIME_BLOCK)
    m = bt * t
    x_spec = pl.BlockSpec((bt, t, D_MODEL), lambda b, s: (b, s, 0))
    st_spec = pl.BlockSpec((bt, 2, 2 * D_FF), lambda b, s: (b, 0, 0))
    return pl.pallas_call(
        functools.partial(_ffn_kernel, bt=bt, t=t, final=final),
        grid=(batch // bt, length // t),
        in_specs=[x_spec, _state_spec(st0, layer, bt)]
        + [_layer_spec(w, layer) for w in (w_up, conv_w, w_down, norm)]
        + [_const_spec(fnorm.shape)],
        out_specs=[x_spec, st_spec],
        out_shape=[jax.ShapeDtypeStruct(x.shape, F32),
                   jax.ShapeDtypeStruct(st0.shape[1:], F32)],
        scratch_shapes=[pltpu.VMEM((D_MODEL // LANES, m, LANES), F32),
                        pltpu.VMEM((m, D_MODEL), BF16),
                        pltpu.VMEM((m, D_FF), BF16)],
        compiler_params=pltpu.CompilerParams(
            dimension_semantics=("arbitrary", "arbitrary"),
            vmem_limit_bytes=VMEM_LIMIT_BYTES),
        name="ffn",
    )(x, st0, w_up, conv_w, w_down, norm, fnorm)


def _trunk(x, conv_s, gla_s, ffn_s, p):
    depth = conv_s.shape[0]
    new_c, new_g, new_f = [], [], []
    for l in range(depth):
        x, c, g = _mixer_call(x, l, conv_s, gla_s, p["w_in"], p["w_alpha"], p["b_alpha"],
                              p["conv_w"], p["g_conv"], p["g_gla"], p["w_out"], p["norm1"])
        x, f = _ffn_call(x, l, ffn_s, p["w_up"], p["ffn_conv_w"], p["w_down"],
                         p["norm2"], p["final_norm"], final=(l == depth - 1))
        new_c.append(c)
        new_g.append(g)
        new_f.append(f)
    return x, jnp.stack(new_c), jnp.stack(new_g), jnp.stack(new_f)


def kernel(x_prompt, x_sample, state_conv, state_gla, state_ffn_conv, w_in, w_alpha, b_alpha, conv_w, g_conv, g_gla, w_out, norm1, norm2, w_up, ffn_conv_w, w_down, final_norm):
    depth = w_in.shape[0]
    gate_cols = w_in[:, :, C_R:C_R + GATE_RANK]
    w_in_r = jnp.concatenate(
        [w_in[:, :, :C_R], w_in[:, :, C_R + GATE_RANK:],
         jnp.pad(gate_cols, ((0, 0), (0, 0), (0, LANES - GATE_RANK)))], axis=-1).astype(BF16)
    w_alpha_p = jnp.pad(w_alpha, ((0, 0), (0, LANES - GATE_RANK), (0, 0))).astype(BF16)
    p = {
        "w_in": w_in_r, "w_alpha": w_alpha_p,
        "b_alpha": b_alpha.reshape(depth, 1, D_QK),
        "conv_w": conv_w, "g_conv": g_conv.reshape(depth, 1, D_CONV),
        "g_gla": g_gla.reshape(depth, 1, HEAD_V),
        "w_out": w_out.astype(BF16),
        "norm1": norm1.reshape(depth, 1, D_MODEL), "norm2": norm2.reshape(depth, 1, D_MODEL),
        "w_up": w_up.astype(BF16), "ffn_conv_w": ffn_conv_w, "w_down": w_down.astype(BF16),
        "final_norm": final_norm.reshape(1, D_MODEL),
    }
    bp = x_prompt.shape[0]
    dt = x_prompt.dtype
    zc = jnp.zeros((depth, bp) + state_conv.shape[2:], dt)
    zg = jnp.zeros((depth, bp) + state_gla.shape[2:], dt)
    zf = jnp.zeros((depth, bp) + state_ffn_conv.shape[2:], dt)
    y_prompt, pc, pg, pf = _trunk(x_prompt, zc, zg, zf, p)
    y_sample, sc, sg, sf = _trunk(x_sample, state_conv, state_gla, state_ffn_conv, p)
    return (y_prompt, y_sample, pc, pg, pf, sc, sg, sf)
```

```python
import functools

import jax
import jax.numpy as jnp
from jax import lax
from jax.experimental import pallas as pl
from jax.experimental.pallas import tpu as pltpu

F32 = jnp.float32
BF16 = jnp.bfloat16

D_MODEL = 1024
D_CONV = 512
N_HEADS = 4
HEAD_K = 64
HEAD_V = 128
D_QK = N_HEADS * HEAD_K
D_GLA = N_HEADS * HEAD_V
GATE_RANK = 16
GATE_TAU = 16.0
D_FF = 2816
CHUNK = 64
EPS = 1e-6

LANES = 128
SUBLANES = 8
C_BG, C_CG, C_H = 0, D_CONV, 2 * D_CONV
C_Q = 3 * D_CONV
C_K = C_Q + D_QK
C_V = C_K + D_QK
C_R = C_V + D_GLA
C_GR = C_R + D_GLA
D_IN_PAD = C_GR + LANES

FFN_COL_BLOCK = 256
FFN_TIME_BLOCK = 1024
MIXER_TIME_BLOCK = 1024
MIXER_ROW_BLOCK = 256
MIXER_PROJ_COLS = 512
VMEM_LIMIT_BYTES = 56 * 1024 * 1024


def _rms_scale(v):
    return lax.rsqrt(jnp.mean(v * v, axis=-1, keepdims=True) + EPS)


def _log_sigmoid(v):
    return jnp.minimum(v, 0.0) - jnp.log1p(jnp.exp(-jnp.abs(v)))


def _causal_taps(u, prev, w):
    t = u.shape[0]
    row = lax.broadcasted_iota(jnp.int32, (t, 1), 0)
    s1 = jnp.where(row == 0, prev[1:2], pltpu.roll(u, 1, 0))
    s2 = jnp.where(row == 0, prev[0:1], jnp.where(row == 1, prev[1:2], pltpu.roll(u, 2, 0)))
    return s2 * w[0:1] + s1 * w[1:2] + u * w[2:3]


def _causal_taps_interleaved(u, prev, w):
    t = u.shape[0]
    sub = lax.broadcasted_iota(jnp.int32, (SUBLANES, 1), 0)
    before1 = jnp.where(sub == 0, prev[1:2], pltpu.roll(u[t - SUBLANES:t], 1, 0))
    before2 = jnp.where(sub == 0, prev[0:1],
                        pltpu.roll(u[t - 2 * SUBLANES:t - SUBLANES], 1, 0))
    s1 = jnp.concatenate([before1, u[:t - SUBLANES]], axis=0)
    s2 = jnp.concatenate([before2, before1, u[:t - 2 * SUBLANES]], axis=0)
    return s2 * w[0:1] + s1 * w[1:2] + u * w[2:3]


def _half_lane_mask(rows, head):
    lane = lax.broadcasted_iota(jnp.int32, (rows, LANES), 1)
    return (lane // HEAD_K) == (head % 2)


def _block_rows(ref, r0, n, t):
    if t >= n:
        return ref[r0 // t, r0 % t:r0 % t + n, :]
    return ref[r0 // t:(r0 + n) // t].reshape(n, ref.shape[-1])


def _store_block_rows(ref, r0, n, t, val):
    if t >= n:
        ref[r0 // t, r0 % t:r0 % t + n, :] = val
    else:
        ref[r0 // t:(r0 + n) // t] = val.reshape(n // t, t, ref.shape[-1])


def _mixer_kernel(x_ref, conv0_ref, gla0_ref, w_in_ref, w_alpha_ref, b_alpha_ref, conv_w_ref,
                  g_conv_ref, g_gla_ref, w_out_ref, norm_ref,
                  y_ref, conv_ref, gla_ref,
                  xn_ref, z_ref, qt_ref, kt_ref, kd_ref, v_ref, o_ref, ycat_ref, st_ref,
                  *, bt, t, rb):
    m = bt * t
    ncs = rb // CHUNK
    s = pl.program_id(1)
    last = pl.num_programs(1) - 1

    @pl.when(s == 0)
    def _():
        conv_ref[...] = conv0_ref[...]
        kv_mask = (lax.broadcasted_iota(jnp.int32, (D_QK, D_GLA), 0) // HEAD_K
                   == lax.broadcasted_iota(jnp.int32, (D_QK, D_GLA), 1) // HEAD_V)
        for b in range(bt):
            s0 = gla0_ref[b].reshape(D_QK, HEAD_V)
            s0 = jnp.where(kv_mask, jnp.concatenate([s0] * N_HEADS, axis=1), 0.0)
            st_ref[b] = s0.T

    causal = (lax.broadcasted_iota(jnp.int32, (CHUNK, D_QK), 1) % CHUNK
              <= lax.broadcasted_iota(jnp.int32, (CHUNK, D_QK), 0))
    zero_kk = jnp.zeros((CHUNK, LANES), BF16)
    zero_vv = jnp.zeros((CHUNK, HEAD_V), BF16)
    zero_st = jnp.zeros((HEAD_V, LANES), BF16)

    def projection_pieces(r0):
        slot = (r0 // rb) % 2
        x = _block_rows(x_ref, r0, rb, t)
        xn_ref[slot] = (x * _rms_scale(x) * norm_ref[...]).astype(BF16)

        def piece(c0):
            c1 = min(c0 + MIXER_PROJ_COLS, D_IN_PAD)
            z_ref[slot, :, c0:c1] = jnp.dot(xn_ref[slot], w_in_ref[:, c0:c1],
                                            preferred_element_type=F32)
        return [functools.partial(piece, c0) for c0 in range(0, D_IN_PAD, MIXER_PROJ_COLS)]

    def run(pieces, n):
        for _ in range(min(n, len(pieces))):
            pieces.pop(0)()

    first = projection_pieces(0)
    run(first, len(first))
    pending_out = None
    for r0 in range(0, m, rb):
        slot = (r0 // rb) % 2
        z = z_ref.at[slot]
        qt_s, kt_s, kd_s, v_s = qt_ref.at[slot], kt_ref.at[slot], kd_ref.at[slot], v_ref.at[slot]
        o_s, ycat_s = o_ref.at[slot], ycat_ref.at[slot]
        ahead = projection_pieces(r0 + rb) if r0 + rb < m else []

        gate = jnp.dot(z[:, C_GR:C_GR + LANES].astype(BF16), w_alpha_ref[...],
                       preferred_element_type=F32) + b_alpha_ref[...]
        if pending_out is not None:
            pending_out()
        run(ahead, 2)

        bcum = _log_sigmoid(gate) * (1.0 / GATE_TAU)
        pos = lax.broadcasted_iota(jnp.int32, (rb, 1), 0) % CHUNK
        shift = 1
        while shift < CHUNK:
            bcum = bcum + jnp.where(pos >= shift, pltpu.roll(bcum, shift, 0), 0.0)
            shift *= 2
        b3 = bcum.reshape(ncs, CHUNK, D_QK)
        b_last = b3[:, CHUNK - 1:CHUNK, :]
        q = z[:, C_Q:C_Q + D_QK] * (HEAD_K ** -0.5)
        k = z[:, C_K:C_K + D_QK]
        qt_s[...] = (q * jnp.exp(bcum)).astype(BF16)
        kt_s[...] = (k * jnp.exp(-bcum)).astype(BF16)
        kd_s[...] = (k.reshape(ncs, CHUNK, D_QK)
                     * jnp.exp(b_last - b3)).reshape(rb, D_QK).astype(BF16)
        dec = jnp.exp(b_last)
        v_s[...] = z[:, C_V:C_V + D_GLA].astype(BF16)

        for c in range(ncs):
            b = (r0 + c * CHUNK) // t
            crow = slice(c * CHUNK, (c + 1) * CHUNK)
            qt = qt_s[crow, :]
            kt = kt_s[crow, :]
            kd = kd_s[crow, :]
            v = v_s[crow, :]
            k_bd = jnp.concatenate([
                jnp.concatenate([
                    jnp.where(_half_lane_mask(CHUNK, h), kt[:, g * LANES:(g + 1) * LANES], 0.0)
                    if g == h // 2 else zero_kk for g in range(D_QK // LANES)], axis=1)
                for h in range(N_HEADS)], axis=0)
            att = lax.dot_general(qt, k_bd, (((1,), (1,)), ((), ())), preferred_element_type=F32)
            ds_t = lax.dot_general(v, kd, (((0,), (0,)), ((), ())), preferred_element_type=F32)
            run(ahead, 1)
            att = jnp.where(causal, att, 0.0)
            v_bd = jnp.concatenate([
                jnp.concatenate([v[:, h * HEAD_V:(h + 1) * HEAD_V] if g == h else zero_vv
                                 for g in range(N_HEADS)], axis=1)
                for h in range(N_HEADS)], axis=0)
            st_blocks = [st_ref[b, h * HEAD_V:(h + 1) * HEAD_V,
                                (h // 2) * LANES:(h // 2 + 1) * LANES] for h in range(N_HEADS)]
            st_bf = jnp.concatenate([
                jnp.concatenate([st_blocks[h].astype(BF16) if g == h // 2 else zero_st
                                 for g in range(D_QK // LANES)], axis=1)
                for h in range(N_HEADS)], axis=0)
            o = jnp.dot(att.astype(BF16), v_bd, preferred_element_type=F32)
            o = o + lax.dot_general(qt, st_bf, (((1,), (1,)), ((), ())),
                                    preferred_element_type=F32)
            o_s[crow, :] = o
            for h in range(N_HEADS):
                g = h // 2
                upd = jnp.where(_half_lane_mask(HEAD_V, h),
                                ds_t[h * HEAD_V:(h + 1) * HEAD_V, g * LANES:(g + 1) * LANES], 0.0)
                st_ref[b, h * HEAD_V:(h + 1) * HEAD_V, g * LANES:(g + 1) * LANES] = (
                    st_blocks[h] * dec[c, :, g * LANES:(g + 1) * LANES] + upd)

        piece = min(t, rb)
        for p0 in range(0, rb, piece):
            b = (r0 + p0) // t
            prow = slice(p0, p0 + piece)
            ch = z[prow, C_CG:C_CG + D_CONV] * z[prow, C_H:C_H + D_CONV]
            cv = _causal_taps(ch, conv_ref[b], conv_w_ref[...])
            conv_ref[b] = ch[piece - 2:piece]
            yc = z[prow, C_BG:C_BG + D_CONV] * cv
            ycat_s[prow, 0:D_CONV] = (yc * _rms_scale(yc) * g_conv_ref[...]).astype(BF16)

        run(ahead, len(ahead))
        for h in range(N_HEADS):
            o = o_s[:, h * HEAD_V:(h + 1) * HEAD_V]
            r = z[:, C_R + h * HEAD_V:C_R + (h + 1) * HEAD_V]
            o = o * _rms_scale(o) * g_gla_ref[...]
            ycat_s[:, D_CONV + h * HEAD_V:D_CONV + (h + 1) * HEAD_V] = (
                o * (r * jax.nn.sigmoid(r))).astype(BF16)

        def project_out(r0=r0, ycat_s=ycat_s):
            y = jnp.dot(ycat_s[...], w_out_ref[...], preferred_element_type=F32)
            _store_block_rows(y_ref, r0, rb, t, _block_rows(x_ref, r0, rb, t) + y)
        pending_out = project_out
    pending_out()

    @pl.when(s == last)
    def _():
        for b in range(bt):
            full = st_ref[b].T
            for h in range(N_HEADS):
                gla_ref[b, h] = full[h * HEAD_K:(h + 1) * HEAD_K, h * HEAD_V:(h + 1) * HEAD_V]


def _ffn_kernel(x_ref, st0_ref, w_up_ref, conv_w_ref, w_down_ref, norm_ref, fnorm_ref,
                y_ref, st_ref, perm_ref, xn_ref, hid_ref, *, bt, t, final):
    m = bt * t
    s = pl.program_id(1)
    g = t // SUBLANES
    n_slabs = D_MODEL // LANES

    @pl.when(s == 0)
    def _():
        st_ref[...] = st0_ref[...]

    for b in range(bt):
        x = x_ref[b]
        xn = x * _rms_scale(x) * norm_ref[...]
        for a in range(SUBLANES):
            for l in range(n_slabs):
                perm_ref[l, pl.ds(b * t + a, g, stride=SUBLANES), :] = (
                    xn[a * g:(a + 1) * g, l * LANES:(l + 1) * LANES])
    xn_ref[...] = jnp.concatenate([perm_ref[l] for l in range(n_slabs)], axis=1).astype(BF16)

    for j in range(D_FF // FFN_COL_BLOCK):
        halves = []
        for base in (0, D_FF):
            cols = slice(base + j * FFN_COL_BLOCK, base + (j + 1) * FFN_COL_BLOCK)
            u = jnp.dot(xn_ref[...], w_up_ref[:, cols], preferred_element_type=F32)
            w = conv_w_ref[:, cols]
            parts = []
            for b in range(bt):
                ub = u[b * t:(b + 1) * t]
                parts.append(_causal_taps_interleaved(ub, st_ref[b, :, cols], w))
                st_ref[b, 0:1, cols] = ub[t - SUBLANES - 1:t - SUBLANES]
                st_ref[b, 1:2, cols] = ub[t - 1:t]
            halves.append(parts[0] if bt == 1 else jnp.concatenate(parts, axis=0))
        a, gate = halves
        hid_ref[:, j * FFN_COL_BLOCK:(j + 1) * FFN_COL_BLOCK] = (
            gate * jax.nn.sigmoid(gate) * a).astype(BF16)

    y = jnp.dot(hid_ref[...], w_down_ref[...], preferred_element_type=F32)
    for l in range(n_slabs):
        perm_ref[l] = y[:, l * LANES:(l + 1) * LANES]
    for b in range(bt):
        y = jnp.concatenate([
            jnp.concatenate([perm_ref[l, pl.ds(b * t + a, g, stride=SUBLANES), :]
                             for l in range(n_slabs)], axis=1)
            for a in range(SUBLANES)], axis=0)
        y = x_ref[b] + y
        if final:
            y = y * _rms_scale(y) * fnorm_ref[...]
        y_ref[b] = y


def _const_spec(shape):
    return pl.BlockSpec(shape, lambda b, s: (0,) * len(shape), pipeline_mode=pl.Buffered(1))


def _layer_spec(stacked, layer):
    rest = stacked.shape[1:]
    return pl.BlockSpec((None,) + rest, lambda b, s: (layer,) + (0,) * len(rest),
                        pipeline_mode=pl.Buffered(1))


def _state_spec(stacked, layer, bt):
    rest = stacked.shape[2:]
    return pl.BlockSpec((None, bt) + rest, lambda b, s: (layer, b) + (0,) * len(rest))


def _block_sizes(batch, length, time_block):
    t = min(length, time_block)
    bt = max(1, min(batch, 256 // t))
    assert length % t == 0 and batch % bt == 0 and t % CHUNK == 0
    return bt, t


def _mixer_call(x, layer, conv0, gla0, w_in, w_alpha, b_alpha, conv_w, g_conv, g_gla, w_out, norm):
    batch, length, _ = x.shape
    bt, t = _block_sizes(batch, length, MIXER_TIME_BLOCK)
    m = bt * t
    rb = min(m, MIXER_ROW_BLOCK)
    assert m % rb == 0 and (t % rb == 0 or rb % t == 0)
    x_spec = pl.BlockSpec((bt, t, D_MODEL), lambda b, s: (b, s, 0))
    conv_spec = pl.BlockSpec((bt, 2, D_CONV), lambda b, s: (b, 0, 0))
    gla_spec = pl.BlockSpec((bt, N_HEADS, HEAD_K, HEAD_V), lambda b, s: (b, 0, 0, 0))
    return pl.pallas_call(
        functools.partial(_mixer_kernel, bt=bt, t=t, rb=rb),
        grid=(batch // bt, length // t),
        in_specs=[x_spec, _state_spec(conv0, layer, bt), _state_spec(gla0, layer, bt)]
        + [_layer_spec(w, layer)
           for w in (w_in, w_alpha, b_alpha, conv_w, g_conv, g_gla, w_out, norm)],
        out_specs=[x_spec, conv_spec, gla_spec],
        out_shape=[jax.ShapeDtypeStruct(x.shape, F32),
                   jax.ShapeDtypeStruct(conv0.shape[1:], F32),
                   jax.ShapeDtypeStruct(gla0.shape[1:], F32)],
        scratch_shapes=[
            pltpu.VMEM((2, rb, D_MODEL), BF16),
            pltpu.VMEM((2, rb, D_IN_PAD), F32),
            pltpu.VMEM((2, rb, D_QK), BF16),
            pltpu.VMEM((2, rb, D_QK), BF16),
            pltpu.VMEM((2, rb, D_QK), BF16),
            pltpu.VMEM((2, rb, D_GLA), BF16),
            pltpu.VMEM((2, rb, D_GLA), F32),
            pltpu.VMEM((2, rb, D_MODEL), BF16),
            pltpu.VMEM((bt, D_GLA, D_QK), F32),
        ],
        compiler_params=pltpu.CompilerParams(
            dimension_semantics=("arbitrary", "arbitrary"),
            vmem_limit_bytes=VMEM_LIMIT_BYTES),
        name="mixer",
    )(x, conv0, gla0, w_in, w_alpha, b_alpha, conv_w, g_conv, g_gla, w_out, norm)


def _ffn_call(x, layer, st0, w_up, conv_w, w_down, norm, fnorm, final):
    batch, length, _ = x.shape
    bt, t = _block_sizes(batch, length, FFN_TIME_BLOCK)
    m = bt * t
    x_spec = pl.BlockSpec((bt, t, D_MODEL), lambda b, s: (b, s, 0))
    st_spec = pl.BlockSpec((bt, 2, 2 * D_FF), lambda b, s: (b, 0, 0))
    return pl.pallas_call(
        functools.partial(_ffn_kernel, bt=bt, t=t, final=final),
        grid=(batch // bt, length // t),
        in_specs=[x_spec, _state_spec(st0, layer, bt)]
        + [_layer_spec(w, layer) for w in (w_up, conv_w, w_down, norm)]
        + [_const_spec(fnorm.shape)],
        out_specs=[x_spec, st_spec],
        out_shape=[jax.ShapeDtypeStruct(x.shape, F32),
                   jax.ShapeDtypeStruct(st0.shape[1:], F32)],
        scratch_shapes=[pltpu.VMEM((D_MODEL // LANES, m, LANES), F32),
                        pltpu.VMEM((m, D_MODEL), BF16),
                        pltpu.VMEM((m, D_FF), BF16)],
        compiler_params=pltpu.CompilerParams(
            dimension_semantics=("arbitrary", "arbitrary"),
            vmem_limit_bytes=VMEM_LIMIT_BYTES),
        name="ffn",
    )(x, st0, w_up, conv_w, w_down, norm, fnorm)


def _trunk(x, conv_s, gla_s, ffn_s, p):
    depth = conv_s.shape[0]
    new_c, new_g, new_f = [], [], []
    for l in range(depth):
        x, c, g = _mixer_call(x, l, conv_s, gla_s, p["w_in"], p["w_alpha"], p["b_alpha"],
                              p["conv_w"], p["g_conv"], p["g_gla"], p["w_out"], p["norm1"])
        x, f = _ffn_call(x, l, ffn_s, p["w_up"], p["ffn_conv_w"], p["w_down"],
                         p["norm2"], p["final_norm"], final=(l == depth - 1))
        new_c.append(c)
        new_g.append(g)
        new_f.append(f)
    return x, jnp.stack(new_c), jnp.stack(new_g), jnp.stack(new_f)


def kernel(x_prompt, x_sample, state_conv, state_gla, state_ffn_conv, w_in, w_alpha, b_alpha, conv_w, g_conv, g_gla, w_out, norm1, norm2, w_up, ffn_conv_w, w_down, final_norm):
    depth = w_in.shape[0]
    gate_cols = w_in[:, :, C_R:C_R + GATE_RANK]
    w_in_r = jnp.concatenate(
        [w_in[:, :, :C_R], w_in[:, :, C_R + GATE_RANK:],
         jnp.pad(gate_cols, ((0, 0), (0, 0), (0, LANES - GATE_RANK)))], axis=-1).astype(BF16)
    w_alpha_p = jnp.pad(w_alpha, ((0, 0), (0, LANES - GATE_RANK), (0, 0))).astype(BF16)
    p = {
        "w_in": w_in_r, "w_alpha": w_alpha_p,
        "b_alpha": b_alpha.reshape(depth, 1, D_QK),
        "conv_w": conv_w, "g_conv": g_conv.reshape(depth, 1, D_CONV),
        "g_gla": g_gla.reshape(depth, 1, HEAD_V),
        "w_out": w_out.astype(BF16),
        "norm1": norm1.reshape(depth, 1, D_MODEL), "norm2": norm2.reshape(depth, 1, D_MODEL),
        "w_up": w_up.astype(BF16), "ffn_conv_w": ffn_conv_w, "w_down": w_down.astype(BF16),
        "final_norm": final_norm.reshape(1, D_MODEL),
    }
    bp = x_prompt.shape[0]
    dt = x_prompt.dtype
    zc = jnp.zeros((depth, bp) + state_conv.shape[2:], dt)
    zg = jnp.zeros((depth, bp) + state_gla.shape[2:], dt)
    zf = jnp.zeros((depth, bp) + state_ffn_conv.shape[2:], dt)
    y_prompt, pc, pg, pf = _trunk(x_prompt, zc, zg, zf, p)
    y_sample, sc, sg, sf = _trunk(x_sample, state_conv, state_gla, state_ffn_conv, p)
    return (y_prompt, y_sample, pc, pg, pf, sc, sg, sf)
```

```python
import functools

import jax
import jax.numpy as jnp
from jax import lax
from jax.experimental import pallas as pl
from jax.experimental.pallas import tpu as pltpu

F32 = jnp.float32
BF16 = jnp.bfloat16

D_MODEL = 1024
D_CONV = 512
N_HEADS = 4
HEAD_K = 64
HEAD_V = 128
D_QK = N_HEADS * HEAD_K
D_GLA = N_HEADS * HEAD_V
GATE_RANK = 16
GATE_TAU = 16.0
D_FF = 2816
CHUNK = 64
EPS = 1e-6

LANES = 128
SUBLANES = 8
C_BG, C_CG, C_H = 0, D_CONV, 2 * D_CONV
C_Q = 3 * D_CONV
C_K = C_Q + D_QK
C_V = C_K + D_QK
C_R = C_V + D_GLA
C_GR = C_R + D_GLA
D_IN_PAD = C_GR + LANES

FFN_COL_BLOCK = 256
FFN_TIME_BLOCK = 1024
MIXER_TIME_BLOCK = 1024
MIXER_ROW_BLOCK = 256
MIXER_PROJ_COLS = 512
VMEM_LIMIT_BYTES = 56 * 1024 * 1024


def _rms_scale(v):
    return lax.rsqrt(jnp.mean(v * v, axis=-1, keepdims=True) + EPS)


def _log_sigmoid(v):
    return jnp.minimum(v, 0.0) - jnp.log1p(jnp.exp(-jnp.abs(v)))


def _causal_taps(u, prev, w):
    t = u.shape[0]
    row = lax.broadcasted_iota(jnp.int32, (t, 1), 0)
    s1 = jnp.where(row == 0, prev[1:2], pltpu.roll(u, 1, 0))
    s2 = jnp.where(row == 0, prev[0:1], jnp.where(row == 1, prev[1:2], pltpu.roll(u, 2, 0)))
    return s2 * w[0:1] + s1 * w[1:2] + u * w[2:3]


def _causal_taps_interleaved(u, prev, w):
    t = u.shape[0]
    sub = lax.broadcasted_iota(jnp.int32, (SUBLANES, 1), 0)
    before1 = jnp.where(sub == 0, prev[1:2], pltpu.roll(u[t - SUBLANES:t], 1, 0))
    before2 = jnp.where(sub == 0, prev[0:1],
                        pltpu.roll(u[t - 2 * SUBLANES:t - SUBLANES], 1, 0))
    s1 = jnp.concatenate([before1, u[:t - SUBLANES]], axis=0)
    s2 = jnp.concatenate([before2, before1, u[:t - 2 * SUBLANES]], axis=0)
    return s2 * w[0:1] + s1 * w[1:2] + u * w[2:3]


def _half_lane_mask(rows, head):
    lane = lax.broadcasted_iota(jnp.int32, (rows, LANES), 1)
    return (lane // HEAD_K) == (head % 2)


def _block_rows(ref, r0, n, t):
    if t >= n:
        return ref[r0 // t, r0 % t:r0 % t + n, :]
    return ref[r0 // t:(r0 + n) // t].reshape(n, ref.shape[-1])


def _store_block_rows(ref, r0, n, t, val):
    if t >= n:
        ref[r0 // t, r0 % t:r0 % t + n, :] = val
    else:
        ref[r0 // t:(r0 + n) // t] = val.reshape(n // t, t, ref.shape[-1])


def _mixer_kernel(x_ref, conv0_ref, gla0_ref, w_in_ref, w_alpha_ref, b_alpha_ref, conv_w_ref,
                  g_conv_ref, g_gla_ref, w_out_ref, norm_ref,
                  y_ref, conv_ref, gla_ref,
                  xn_ref, z_ref, qt_ref, kt_ref, kd_ref, v_ref, o_ref, ycat_ref,
                  *, bt, t, rb):
    m = bt * t
    ncs = rb // CHUNK
    s = pl.program_id(1)

    @pl.when(s == 0)
    def _():
        conv_ref[...] = conv0_ref[...]
        gla_ref[...] = gla0_ref[...]

    causal = (lax.broadcasted_iota(jnp.int32, (CHUNK, D_QK), 1) % CHUNK
              <= lax.broadcasted_iota(jnp.int32, (CHUNK, D_QK), 0))
    zero_kk = jnp.zeros((CHUNK, LANES), BF16)
    zero_vv = jnp.zeros((CHUNK, HEAD_V), BF16)
    zero_kv = jnp.zeros((HEAD_K, HEAD_V), BF16)

    def projection_pieces(r0):
        slot = (r0 // rb) % 2
        x = _block_rows(x_ref, r0, rb, t)
        xn_ref[slot] = (x * _rms_scale(x) * norm_ref[...]).astype(BF16)

        def piece(c0):
            c1 = min(c0 + MIXER_PROJ_COLS, D_IN_PAD)
            z_ref[slot, :, c0:c1] = jnp.dot(xn_ref[slot], w_in_ref[:, c0:c1],
                                            preferred_element_type=F32)
        return [functools.partial(piece, c0) for c0 in range(0, D_IN_PAD, MIXER_PROJ_COLS)]

    def run(pieces, n):
        for _ in range(min(n, len(pieces))):
            pieces.pop(0)()

    first = projection_pieces(0)
    run(first, len(first))
    pending_out = None
    for r0 in range(0, m, rb):
        slot = (r0 // rb) % 2
        z = z_ref.at[slot]
        qt_s, kt_s, kd_s, v_s = qt_ref.at[slot], kt_ref.at[slot], kd_ref.at[slot], v_ref.at[slot]
        o_s, ycat_s = o_ref.at[slot], ycat_ref.at[slot]
        ahead = projection_pieces(r0 + rb) if r0 + rb < m else []

        gate = jnp.dot(z[:, C_GR:C_GR + LANES].astype(BF16), w_alpha_ref[...],
                       preferred_element_type=F32) + b_alpha_ref[...]
        if pending_out is not None:
            pending_out()
        run(ahead, 2)

        bcum = _log_sigmoid(gate) * (1.0 / GATE_TAU)
        pos = lax.broadcasted_iota(jnp.int32, (rb, 1), 0) % CHUNK
        shift = 1
        while shift < CHUNK:
            bcum = bcum + jnp.where(pos >= shift, pltpu.roll(bcum, shift, 0), 0.0)
            shift *= 2
        b3 = bcum.reshape(ncs, CHUNK, D_QK)
        b_last = b3[:, CHUNK - 1:CHUNK, :]
        q = z[:, C_Q:C_Q + D_QK] * (HEAD_K ** -0.5)
        k = z[:, C_K:C_K + D_QK]
        qt_s[...] = (q * jnp.exp(bcum)).astype(BF16)
        kt_s[...] = (k * jnp.exp(-bcum)).astype(BF16)
        kd_s[...] = (k.reshape(ncs, CHUNK, D_QK)
                     * jnp.exp(b_last - b3)).reshape(rb, D_QK).astype(BF16)
        dec = jnp.exp(b_last)
        v_s[...] = z[:, C_V:C_V + D_GLA].astype(BF16)

        for c in range(ncs):
            b = (r0 + c * CHUNK) // t
            crow = slice(c * CHUNK, (c + 1) * CHUNK)
            qt = qt_s[crow, :]
            kt = kt_s[crow, :]
            kd = kd_s[crow, :]
            v = v_s[crow, :]
            k_bd = jnp.concatenate([
                jnp.concatenate([
                    jnp.where(_half_lane_mask(CHUNK, h), kt[:, g * LANES:(g + 1) * LANES], 0.0)
                    if g == h // 2 else zero_kk for g in range(D_QK // LANES)], axis=1)
                for h in range(N_HEADS)], axis=0)
            att = lax.dot_general(qt, k_bd, (((1,), (1,)), ((), ())), preferred_element_type=F32)
            ds = lax.dot_general(kd, v, (((0,), (0,)), ((), ())), preferred_element_type=F32)
            run(ahead, 1)
            att = jnp.where(causal, att, 0.0)
            v_bd = jnp.concatenate([
                jnp.concatenate([v[:, h * HEAD_V:(h + 1) * HEAD_V] if g == h else zero_vv
                                 for g in range(N_HEADS)], axis=1)
                for h in range(N_HEADS)], axis=0)
            s_blocks = [gla_ref[b, h] for h in range(N_HEADS)]
            s_bd = jnp.concatenate([
                jnp.concatenate([s_blocks[h].astype(BF16) if g == h else zero_kv
                                 for g in range(N_HEADS)], axis=1)
                for h in range(N_HEADS)], axis=0)
            o = jnp.dot(att.astype(BF16), v_bd, preferred_element_type=F32)
            o = o + jnp.dot(qt, s_bd, preferred_element_type=F32)
            o_s[crow, :] = o
            dec_col = jnp.broadcast_to(dec[c], (LANES, D_QK)).T
            for h in range(N_HEADS):
                gla_ref[b, h] = (s_blocks[h] * dec_col[h * HEAD_K:(h + 1) * HEAD_K, :]
                                 + ds[h * HEAD_K:(h + 1) * HEAD_K, h * HEAD_V:(h + 1) * HEAD_V])

        piece = min(t, rb)
        for p0 in range(0, rb, piece):
            b = (r0 + p0) // t
            prow = slice(p0, p0 + piece)
            ch = z[prow, C_CG:C_CG + D_CONV] * z[prow, C_H:C_H + D_CONV]
            cv = _causal_taps(ch, conv_ref[b], conv_w_ref[...])
            conv_ref[b] = ch[piece - 2:piece]
            yc = z[prow, C_BG:C_BG + D_CONV] * cv
            ycat_s[prow, 0:D_CONV] = (yc * _rms_scale(yc) * g_conv_ref[...]).astype(BF16)

        run(ahead, len(ahead))
        for h in range(N_HEADS):
            o = o_s[:, h * HEAD_V:(h + 1) * HEAD_V]
            r = z[:, C_R + h * HEAD_V:C_R + (h + 1) * HEAD_V]
            o = o * _rms_scale(o) * g_gla_ref[...]
            ycat_s[:, D_CONV + h * HEAD_V:D_CONV + (h + 1) * HEAD_V] = (
                o * (r * jax.nn.sigmoid(r))).astype(BF16)

        def project_out(r0=r0, ycat_s=ycat_s):
            y = jnp.dot(ycat_s[...], w_out_ref[...], preferred_element_type=F32)
            _store_block_rows(y_ref, r0, rb, t, _block_rows(x_ref, r0, rb, t) + y)
        pending_out = project_out
    pending_out()


def _ffn_kernel(x_ref, st0_ref, w_up_ref, conv_w_ref, w_down_ref, norm_ref, fnorm_ref,
                y_ref, st_ref, perm_ref, xn_ref, hid_ref, *, bt, t, final):
    m = bt * t
    s = pl.program_id(1)
    g = t // SUBLANES
    n_slabs = D_MODEL // LANES

    @pl.when(s == 0)
    def _():
        st_ref[...] = st0_ref[...]

    for b in range(bt):
        x = x_ref[b]
        xn = x * _rms_scale(x) * norm_ref[...]
        for a in range(SUBLANES):
            for l in range(n_slabs):
                perm_ref[l, pl.ds(b * t + a, g, stride=SUBLANES), :] = (
                    xn[a * g:(a + 1) * g, l * LANES:(l + 1) * LANES])
    xn_ref[...] = jnp.concatenate([perm_ref[l] for l in range(n_slabs)], axis=1).astype(BF16)

    for j in range(D_FF // FFN_COL_BLOCK):
        halves = []
        for base in (0, D_FF):
            cols = slice(base + j * FFN_COL_BLOCK, base + (j + 1) * FFN_COL_BLOCK)
            u = jnp.dot(xn_ref[...], w_up_ref[:, cols], preferred_element_type=F32)
            w = conv_w_ref[:, cols]
            parts = []
            for b in range(bt):
                ub = u[b * t:(b + 1) * t]
                parts.append(_causal_taps_interleaved(ub, st_ref[b, :, cols], w))
                st_ref[b, 0:1, cols] = ub[t - SUBLANES - 1:t - SUBLANES]
                st_ref[b, 1:2, cols] = ub[t - 1:t]
            halves.append(parts[0] if bt == 1 else jnp.concatenate(parts, axis=0))
        a, gate = halves
        hid_ref[:, j * FFN_COL_BLOCK:(j + 1) * FFN_COL_BLOCK] = (
            gate * jax.nn.sigmoid(gate) * a).astype(BF16)

    y = jnp.dot(hid_ref[...], w_down_ref[...], preferred_element_type=F32)
    for l in range(n_slabs):
        perm_ref[l] = y[:, l * LANES:(l + 1) * LANES]
    for b in range(bt):
        y = jnp.concatenate([
            jnp.concatenate([perm_ref[l, pl.ds(b * t + a, g, stride=SUBLANES), :]
                             for l in range(n_slabs)], axis=1)
            for a in range(SUBLANES)], axis=0)
        y = x_ref[b] + y
        if final:
            y = y * _rms_scale(y) * fnorm_ref[...]
        y_ref[b] = y


def _const_spec(shape):
    return pl.BlockSpec(shape, lambda b, s: (0,) * len(shape), pipeline_mode=pl.Buffered(1))


def _layer_spec(stacked, layer):
    rest = stacked.shape[1:]
    return pl.BlockSpec((None,) + rest, lambda b, s: (layer,) + (0,) * len(rest),
                        pipeline_mode=pl.Buffered(1))


def _state_spec(stacked, layer, bt):
    rest = stacked.shape[2:]
    return pl.BlockSpec((None, bt) + rest, lambda b, s: (layer, b) + (0,) * len(rest))


def _block_sizes(batch, length, time_block):
    t = min(length, time_block)
    bt = max(1, min(batch, 256 // t))
    assert length % t == 0 and batch % bt == 0 and t % CHUNK == 0
    return bt, t


def _mixer_call(x, layer, conv0, gla0, w_in, w_alpha, b_alpha, conv_w, g_conv, g_gla, w_out, norm):
    batch, length, _ = x.shape
    bt, t = _block_sizes(batch, length, MIXER_TIME_BLOCK)
    m = bt * t
    rb = min(m, MIXER_ROW_BLOCK)
    assert m % rb == 0 and (t % rb == 0 or rb % t == 0)
    x_spec = pl.BlockSpec((bt, t, D_MODEL), lambda b, s: (b, s, 0))
    conv_spec = pl.BlockSpec((bt, 2, D_CONV), lambda b, s: (b, 0, 0))
    gla_spec = pl.BlockSpec((bt, N_HEADS, HEAD_K, HEAD_V), lambda b, s: (b, 0, 0, 0))
    return pl.pallas_call(
        functools.partial(_mixer_kernel, bt=bt, t=t, rb=rb),
        grid=(batch // bt, length // t),
        in_specs=[x_spec, _state_spec(conv0, layer, bt), _state_spec(gla0, layer, bt)]
        + [_layer_spec(w, layer)
           for w in (w_in, w_alpha, b_alpha, conv_w, g_conv, g_gla, w_out, norm)],
        out_specs=[x_spec, conv_spec, gla_spec],
        out_shape=[jax.ShapeDtypeStruct(x.shape, F32),
                   jax.ShapeDtypeStruct(conv0.shape[1:], F32),
                   jax.ShapeDtypeStruct(gla0.shape[1:], F32)],
        scratch_shapes=[
            pltpu.VMEM((2, rb, D_MODEL), BF16),
            pltpu.VMEM((2, rb, D_IN_PAD), F32),
            pltpu.VMEM((2, rb, D_QK), BF16),
            pltpu.VMEM((2, rb, D_QK), BF16),
            pltpu.VMEM((2, rb, D_QK), BF16),
            pltpu.VMEM((2, rb, D_GLA), BF16),
            pltpu.VMEM((2, rb, D_GLA), F32),
            pltpu.VMEM((2, rb, D_MODEL), BF16),
        ],
        compiler_params=pltpu.CompilerParams(
            dimension_semantics=("arbitrary", "arbitrary"),
            vmem_limit_bytes=VMEM_LIMIT_BYTES),
        name="mixer",
    )(x, conv0, gla0, w_in, w_alpha, b_alpha, conv_w, g_conv, g_gla, w_out, norm)


def _ffn_call(x, layer, st0, w_up, conv_w, w_down, norm, fnorm, final):
    batch, length, _ = x.shape
    bt, t = _block_sizes(batch, length, FFN_TIME_BLOCK)
    m = bt * t
    x_spec = pl.BlockSpec((bt, t, D_MODEL), lambda b, s: (b, s, 0))
    st_spec = pl.BlockSpec((bt, 2, 2 * D_FF), lambda b, s: (b, 0, 0))
    return pl.pallas_call(
        functools.partial(_ffn_kernel, bt=bt, t=t, final=final),
        grid=(batch // bt, length // t),
        in_specs=[x_spec, _state_spec(st0, layer, bt)]
        + [_layer_spec(w, layer) for w in (w_up, conv_w, w_down, norm)]
        + [_const_spec(fnorm.shape)],
        out_specs=[x_spec, st_spec],
        out_shape=[jax.ShapeDtypeStruct(x.shape, F32),
                   jax.ShapeDtypeStruct(st0.shape[1:], F32)],
        scratch_shapes=[pltpu.VMEM((D_MODEL // LANES, m, LANES), F32),
                        pltpu.VMEM((m, D_MODEL), BF16),
                        pltpu.VMEM((m, D_FF), BF16)],
        compiler_params=pltpu.CompilerParams(
            dimension_semantics=("arbitrary", "arbitrary"),
            vmem_limit_bytes=VMEM_LIMIT_BYTES),
        name="ffn",
    )(x, st0, w_up, conv_w, w_down, norm, fnorm)


def _trunk(x, conv_s, gla_s, ffn_s, p):
    depth = conv_s.shape[0]
    new_c, new_g, new_f = [], [], []
    for l in range(depth):
        x, c, g = _mixer_call(x, l, conv_s, gla_s, p["w_in"], p["w_alpha"], p["b_alpha"],
                              p["conv_w"], p["g_conv"], p["g_gla"], p["w_out"], p["norm1"])
        x, f = _ffn_call(x, l, ffn_s, p["w_up"], p["ffn_conv_w"], p["w_down"],
                         p["norm2"], p["final_norm"], final=(l == depth - 1))
        new_c.append(c)
        new_g.append(g)
        new_f.append(f)
    return x, jnp.stack(new_c), jnp.stack(new_g), jnp.stack(new_f)


def kernel(x_prompt, x_sample, state_conv, state_gla, state_ffn_conv, w_in, w_alpha, b_alpha, conv_w, g_conv, g_gla, w_out, norm1, norm2, w_up, ffn_conv_w, w_down, final_norm):
    depth = w_in.shape[0]
    gate_cols = w_in[:, :, C_R:C_R + GATE_RANK]
    w_in_r = jnp.concatenate(
        [w_in[:, :, :C_R], w_in[:, :, C_R + GATE_RANK:],
         jnp.pad(gate_cols, ((0, 0), (0, 0), (0, LANES - GATE_RANK)))], axis=-1).astype(BF16)
    w_alpha_p = jnp.pad(w_alpha, ((0, 0), (0, LANES - GATE_RANK), (0, 0))).astype(BF16)
    p = {
        "w_in": w_in_r, "w_alpha": w_alpha_p,
        "b_alpha": b_alpha.reshape(depth, 1, D_QK),
        "conv_w": conv_w, "g_conv": g_conv.reshape(depth, 1, D_CONV),
        "g_gla": g_gla.reshape(depth, 1, HEAD_V),
        "w_out": w_out.astype(BF16),
        "norm1": norm1.reshape(depth, 1, D_MODEL), "norm2": norm2.reshape(depth, 1, D_MODEL),
        "w_up": w_up.astype(BF16), "ffn_conv_w": ffn_conv_w, "w_down": w_down.astype(BF16),
        "final_norm": final_norm.reshape(1, D_MODEL),
    }
    bp = x_prompt.shape[0]
    dt = x_prompt.dtype
    zc = jnp.zeros((depth, bp) + state_conv.shape[2:], dt)
    zg = jnp.zeros((depth, bp) + state_gla.shape[2:], dt)
    zf = jnp.zeros((depth, bp) + state_ffn_conv.shape[2:], dt)
    y_prompt, pc, pg, pf = _trunk(x_prompt, zc, zg, zf, p)
    y_sample, sc, sg, sf = _trunk(x_sample, state_conv, state_gla, state_ffn_conv, p)
    return (y_prompt, y_sample, pc, pg, pf, sc, sg, sf)
```

```python
import functools

import jax
import jax.numpy as jnp
from jax import lax
from jax.experimental import pallas as pl
from jax.experimental.pallas import tpu as pltpu

F32 = jnp.float32
BF16 = jnp.bfloat16

D_MODEL = 1024
D_CONV = 512
N_HEADS = 4
HEAD_K = 64
HEAD_V = 128
D_QK = N_HEADS * HEAD_K
D_GLA = N_HEADS * HEAD_V
GATE_RANK = 16
GATE_TAU = 16.0
D_FF = 2816
CHUNK = 64
EPS = 1e-6

LANES = 128
SUBLANES = 8
C_BG, C_CG, C_H = 0, D_CONV, 2 * D_CONV
C_Q = 3 * D_CONV
C_K = C_Q + D_QK
C_V = C_K + D_QK
C_R = C_V + D_GLA
C_GR = C_R + D_GLA
D_IN_PAD = C_GR + LANES

FFN_COL_BLOCK = 256
FFN_TIME_BLOCK = 1024
MIXER_TIME_BLOCK = 1024
MIXER_ROW_BLOCK = 256
MIXER_PROJ_COLS = 512
VMEM_LIMIT_BYTES = 56 * 1024 * 1024


def _rms_scale(v):
    return lax.rsqrt(jnp.mean(v * v, axis=-1, keepdims=True) + EPS)


def _log_sigmoid(v):
    return jnp.minimum(v, 0.0) - jnp.log1p(jnp.exp(-jnp.abs(v)))


def _causal_taps(u, prev, w):
    t = u.shape[0]
    row = lax.broadcasted_iota(jnp.int32, (t, 1), 0)
    s1 = jnp.where(row == 0, prev[1:2], pltpu.roll(u, 1, 0))
    s2 = jnp.where(row == 0, prev[0:1], jnp.where(row == 1, prev[1:2], pltpu.roll(u, 2, 0)))
    return s2 * w[0:1] + s1 * w[1:2] + u * w[2:3]


def _causal_taps_interleaved(u, prev, w):
    t = u.shape[0]
    sub = lax.broadcasted_iota(jnp.int32, (SUBLANES, 1), 0)
    before1 = jnp.where(sub == 0, prev[1:2], pltpu.roll(u[t - SUBLANES:t], 1, 0))
    before2 = jnp.where(sub == 0, prev[0:1],
                        pltpu.roll(u[t - 2 * SUBLANES:t - SUBLANES], 1, 0))
    s1 = jnp.concatenate([before1, u[:t - SUBLANES]], axis=0)
    s2 = jnp.concatenate([before2, before1, u[:t - 2 * SUBLANES]], axis=0)
    return s2 * w[0:1] + s1 * w[1:2] + u * w[2:3]


def _half_lane_mask(rows, head):
    lane = lax.broadcasted_iota(jnp.int32, (rows, LANES), 1)
    return (lane // HEAD_K) == (head % 2)


def _block_rows(ref, r0, n, t):
    if t >= n:
        return ref[r0 // t, r0 % t:r0 % t + n, :]
    return ref[r0 // t:(r0 + n) // t].reshape(n, ref.shape[-1])


def _store_block_rows(ref, r0, n, t, val):
    if t >= n:
        ref[r0 // t, r0 % t:r0 % t + n, :] = val
    else:
        ref[r0 // t:(r0 + n) // t] = val.reshape(n // t, t, ref.shape[-1])


def _mixer_kernel(x_ref, conv0_ref, gla0_ref, w_in_ref, w_alpha_ref, b_alpha_ref, conv_w_ref,
                  g_conv_ref, g_gla_ref, w_out_ref, norm_ref,
                  y_ref, conv_ref, gla_ref,
                  xn_ref, z_ref, qt_ref, kt_ref, kd_ref, v_ref, o_ref, ycat_ref,
                  *, bt, t, rb):
    m = bt * t
    ncs = rb // CHUNK
    s = pl.program_id(1)

    @pl.when(s == 0)
    def _():
        conv_ref[...] = conv0_ref[...]
        gla_ref[...] = gla0_ref[...]

    causal = (lax.broadcasted_iota(jnp.int32, (CHUNK, D_QK), 1) % CHUNK
              <= lax.broadcasted_iota(jnp.int32, (CHUNK, D_QK), 0))
    zero_kk = jnp.zeros((CHUNK, LANES), BF16)
    zero_vv = jnp.zeros((CHUNK, HEAD_V), BF16)
    zero_kv = jnp.zeros((HEAD_K, HEAD_V), BF16)

    def projection_pieces(r0):
        slot = (r0 // rb) % 2
        x = _block_rows(x_ref, r0, rb, t)
        xn_ref[slot] = (x * _rms_scale(x) * norm_ref[...]).astype(BF16)

        def piece(c0):
            c1 = min(c0 + MIXER_PROJ_COLS, D_IN_PAD)
            z_ref[slot, :, c0:c1] = jnp.dot(xn_ref[slot], w_in_ref[:, c0:c1],
                                            preferred_element_type=F32)
        return [functools.partial(piece, c0) for c0 in range(0, D_IN_PAD, MIXER_PROJ_COLS)]

    def run(pieces, n):
        for _ in range(min(n, len(pieces))):
            pieces.pop(0)()

    first = projection_pieces(0)
    run(first, len(first))
    pending_out = None
    for r0 in range(0, m, rb):
        slot = (r0 // rb) % 2
        z = z_ref.at[slot]
        qt_s, kt_s, kd_s, v_s = qt_ref.at[slot], kt_ref.at[slot], kd_ref.at[slot], v_ref.at[slot]
        o_s, ycat_s = o_ref.at[slot], ycat_ref.at[slot]
        ahead = projection_pieces(r0 + rb) if r0 + rb < m else []

        gate = jnp.dot(z[:, C_GR:C_GR + LANES].astype(BF16), w_alpha_ref[...],
                       preferred_element_type=F32) + b_alpha_ref[...]
        if pending_out is not None:
            pending_out()
        run(ahead, 2)

        bcum = _log_sigmoid(gate) * (1.0 / GATE_TAU)
        pos = lax.broadcasted_iota(jnp.int32, (rb, 1), 0) % CHUNK
        shift = 1
        while shift < CHUNK:
            bcum = bcum + jnp.where(pos >= shift, pltpu.roll(bcum, shift, 0), 0.0)
            shift *= 2
        b3 = bcum.reshape(ncs, CHUNK, D_QK)
        b_last = b3[:, CHUNK - 1:CHUNK, :]
        q = z[:, C_Q:C_Q + D_QK] * (HEAD_K ** -0.5)
        k = z[:, C_K:C_K + D_QK]
        qt_s[...] = (q * jnp.exp(bcum)).astype(BF16)
        kt_s[...] = (k * jnp.exp(-bcum)).astype(BF16)
        kd_s[...] = (k.reshape(ncs, CHUNK, D_QK)
                     * jnp.exp(b_last - b3)).reshape(rb, D_QK).astype(BF16)
        dec = jnp.exp(b_last)
        v_s[...] = z[:, C_V:C_V + D_GLA].astype(BF16)

        for c in range(ncs):
            b = (r0 + c * CHUNK) // t
            crow = slice(c * CHUNK, (c + 1) * CHUNK)
            qt = qt_s[crow, :]
            kt = kt_s[crow, :]
            kd = kd_s[crow, :]
            v = v_s[crow, :]
            k_bd = jnp.concatenate([
                jnp.concatenate([
                    jnp.where(_half_lane_mask(CHUNK, h), kt[:, g * LANES:(g + 1) * LANES], 0.0)
                    if g == h // 2 else zero_kk for g in range(D_QK // LANES)], axis=1)
                for h in range(N_HEADS)], axis=0)
            att = lax.dot_general(qt, k_bd, (((1,), (1,)), ((), ())), preferred_element_type=F32)
            ds = lax.dot_general(kd, v, (((0,), (0,)), ((), ())), preferred_element_type=F32)
            run(ahead, 1)
            att = jnp.where(causal, att, 0.0)
            v_bd = jnp.concatenate([
                jnp.concatenate([v[:, h * HEAD_V:(h + 1) * HEAD_V] if g == h else zero_vv
                                 for g in range(N_HEADS)], axis=1)
                for h in range(N_HEADS)], axis=0)
            s_blocks = [gla_ref[b, h] for h in range(N_HEADS)]
            s_bd = jnp.concatenate([
                jnp.concatenate([s_blocks[h].astype(BF16) if g == h else zero_kv
                                 for g in range(N_HEADS)], axis=1)
                for h in range(N_HEADS)], axis=0)
            o = jnp.dot(att.astype(BF16), v_bd, preferred_element_type=F32)
            o = o + jnp.dot(qt, s_bd, preferred_element_type=F32)
            o_s[crow, :] = o
            dec_col = jnp.broadcast_to(dec[c], (LANES, D_QK)).T
            for h in range(N_HEADS):
                gla_ref[b, h] = (s_blocks[h] * dec_col[h * HEAD_K:(h + 1) * HEAD_K, :]
                                 + ds[h * HEAD_K:(h + 1) * HEAD_K, h * HEAD_V:(h + 1) * HEAD_V])

        piece = min(t, rb)
        for p0 in range(0, rb, piece):
            b = (r0 + p0) // t
            prow = slice(p0, p0 + piece)
            ch = z[prow, C_CG:C_CG + D_CONV] * z[prow, C_H:C_H + D_CONV]
            cv = _causal_taps(ch, conv_ref[b], conv_w_ref[...])
            conv_ref[b] = ch[piece - 2:piece]
            yc = z[prow, C_BG:C_BG + D_CONV] * cv
            ycat_s[prow, 0:D_CONV] = (yc * _rms_scale(yc) * g_conv_ref[...]).astype(BF16)

        run(ahead, len(ahead))
        for h in range(N_HEADS):
            o = o_s[:, h * HEAD_V:(h + 1) * HEAD_V]
            r = z[:, C_R + h * HEAD_V:C_R + (h + 1) * HEAD_V]
            o = o * _rms_scale(o) * g_gla_ref[...]
            ycat_s[:, D_CONV + h * HEAD_V:D_CONV + (h + 1) * HEAD_V] = (
                o * (r * jax.nn.sigmoid(r))).astype(BF16)

        def project_out(r0=r0, ycat_s=ycat_s):
            y = jnp.dot(ycat_s[...], w_out_ref[...], preferred_element_type=F32)
            _store_block_rows(y_ref, r0, rb, t, _block_rows(x_ref, r0, rb, t) + y)
        pending_out = project_out
    pending_out()


def _ffn_kernel(x_ref, st0_ref, w_up_ref, conv_w_ref, w_down_ref, norm_ref, fnorm_ref,
                y_ref, st_ref, perm_ref, xn_ref, hid_ref, *, bt, t, final):
    m = bt * t
    s = pl.program_id(1)
    g = t // SUBLANES
    n_slabs = D_MODEL // LANES

    @pl.when(s == 0)
    def _():
        st_ref[...] = st0_ref[...]

    for b in range(bt):
        x = x_ref[b]
        xn = x * _rms_scale(x) * norm_ref[...]
        for a in range(SUBLANES):
            for l in range(n_slabs):
                perm_ref[l, pl.ds(b * t + a, g, stride=SUBLANES), :] = (
                    xn[a * g:(a + 1) * g, l * LANES:(l + 1) * LANES])
    xn_ref[...] = jnp.concatenate([perm_ref[l] for l in range(n_slabs)], axis=1).astype(BF16)

    for j in range(D_FF // FFN_COL_BLOCK):
        halves = []
        for base in (0, D_FF):
            cols = slice(base + j * FFN_COL_BLOCK, base + (j + 1) * FFN_COL_BLOCK)
            u = jnp.dot(xn_ref[...], w_up_ref[:, cols], preferred_element_type=F32)
            w = conv_w_ref[:, cols]
            parts = []
            for b in range(bt):
                ub = u[b * t:(b + 1) * t]
                parts.append(_causal_taps_interleaved(ub, st_ref[b, :, cols], w))
                st_ref[b, 0:1, cols] = ub[t - SUBLANES - 1:t - SUBLANES]
                st_ref[b, 1:2, cols] = ub[t - 1:t]
            halves.append(parts[0] if bt == 1 else jnp.concatenate(parts, axis=0))
        a, gate = halves
        hid_ref[:, j * FFN_COL_BLOCK:(j + 1) * FFN_COL_BLOCK] = (
            gate * jax.nn.sigmoid(gate) * a).astype(BF16)

    y = jnp.dot(hid_ref[...], w_down_ref[...], preferred_element_type=F32)
    for l in range(n_slabs):
        perm_ref[l] = y[:, l * LANES:(l + 1) * LANES]
    for b in range(bt):
        y = jnp.concatenate([
            jnp.concatenate([perm_ref[l, pl.ds(b * t + a, g, stride=SUBLANES), :]
                             for l in range(n_slabs)], axis=1)
            for a in range(SUBLANES)], axis=0)
        y = x_ref[b] + y
        if final:
            y = y * _rms_scale(y) * fnorm_ref[...]
        y_ref[b] = y


def _const_spec(shape):
    return pl.BlockSpec(shape, lambda b, s: (0,) * len(shape), pipeline_mode=pl.Buffered(1))


def _layer_spec(stacked, layer):
    rest = stacked.shape[1:]
    return pl.BlockSpec((None,) + rest, lambda b, s: (layer,) + (0,) * len(rest),
                        pipeline_mode=pl.Buffered(1))


def _state_spec(stacked, layer, bt):
    rest = stacked.shape[2:]
    return pl.BlockSpec((None, bt) + rest, lambda b, s: (layer, b) + (0,) * len(rest))


def _block_sizes(batch, length, time_block):
    t = min(length, time_block)
    bt = max(1, min(batch, time_block // t))
    assert length % t == 0 and batch % bt == 0 and t % CHUNK == 0
    return bt, t


def _mixer_call(x, layer, conv0, gla0, w_in, w_alpha, b_alpha, conv_w, g_conv, g_gla, w_out, norm):
    batch, length, _ = x.shape
    bt, t = _block_sizes(batch, length, MIXER_TIME_BLOCK)
    m = bt * t
    rb = min(m, MIXER_ROW_BLOCK)
    assert m % rb == 0 and (t % rb == 0 or rb % t == 0)
    x_spec = pl.BlockSpec((bt, t, D_MODEL), lambda b, s: (b, s, 0))
    conv_spec = pl.BlockSpec((bt, 2, D_CONV), lambda b, s: (b, 0, 0))
    gla_spec = pl.BlockSpec((bt, N_HEADS, HEAD_K, HEAD_V), lambda b, s: (b, 0, 0, 0))
    return pl.pallas_call(
        functools.partial(_mixer_kernel, bt=bt, t=t, rb=rb),
        grid=(batch // bt, length // t),
        in_specs=[x_spec, _state_spec(conv0, layer, bt), _state_spec(gla0, layer, bt)]
        + [_layer_spec(w, layer)
           for w in (w_in, w_alpha, b_alpha, conv_w, g_conv, g_gla, w_out, norm)],
        out_specs=[x_spec, conv_spec, gla_spec],
        out_shape=[jax.ShapeDtypeStruct(x.shape, F32),
                   jax.ShapeDtypeStruct(conv0.shape[1:], F32),
                   jax.ShapeDtypeStruct(gla0.shape[1:], F32)],
        scratch_shapes=[
            pltpu.VMEM((2, rb, D_MODEL), BF16),
            pltpu.VMEM((2, rb, D_IN_PAD), F32),
            pltpu.VMEM((2, rb, D_QK), BF16),
            pltpu.VMEM((2, rb, D_QK), BF16),
            pltpu.VMEM((2, rb, D_QK), BF16),
            pltpu.VMEM((2, rb, D_GLA), BF16),
            pltpu.VMEM((2, rb, D_GLA), F32),
            pltpu.VMEM((2, rb, D_MODEL), BF16),
        ],
        compiler_params=pltpu.CompilerParams(
            dimension_semantics=("arbitrary", "arbitrary"),
            vmem_limit_bytes=VMEM_LIMIT_BYTES),
        name="mixer",
    )(x, conv0, gla0, w_in, w_alpha, b_alpha, conv_w, g_conv, g_gla, w_out, norm)


def _ffn_call(x, layer, st0, w_up, conv_w, w_down, norm, fnorm, final):
    batch, length, _ = x.shape
    bt, t = _block_sizes(batch, length, FFN_TIME_BLOCK)
    m = bt * t
    x_spec = pl.BlockSpec((bt, t, D_MODEL), lambda b, s: (b, s, 0))
    st_spec = pl.BlockSpec((bt, 2, 2 * D_FF), lambda b, s: (b, 0, 0))
    return pl.pallas_call(
        functools.partial(_ffn_kernel, bt=bt, t=t, final=final),
        grid=(batch // bt, length // t),
        in_specs=[x_spec, _state_spec(st0, layer, bt)]
        + [_layer_spec(w, layer) for w in (w_up, conv_w, w_down, norm)]
        + [_const_spec(fnorm.shape)],
        out_specs=[x_spec, st_spec],
        out_shape=[jax.ShapeDtypeStruct(x.shape, F32),
                   jax.ShapeDtypeStruct(st0.shape[1:], F32)],
        scratch_shapes=[pltpu.VMEM((D_MODEL // LANES, m, LANES), F32),
                        pltpu.VMEM((m, D_MODEL), BF16),
                        pltpu.VMEM((m, D_FF), BF16)],
        compiler_params=pltpu.CompilerParams(
            dimension_semantics=("arbitrary", "arbitrary"),
            vmem_limit_bytes=VMEM_LIMIT_BYTES),
        name="ffn",
    )(x, st0, w_up, conv_w, w_down, norm, fnorm)


def _trunk(x, conv_s, gla_s, ffn_s, p):
    depth = conv_s.shape[0]
    new_c, new_g, new_f = [], [], []
    for l in range(depth):
        x, c, g = _mixer_call(x, l, conv_s, gla_s, p["w_in"], p["w_alpha"], p["b_alpha"],
                              p["conv_w"], p["g_conv"], p["g_gla"], p["w_out"], p["norm1"])
        x, f = _ffn_call(x, l, ffn_s, p["w_up"], p["ffn_conv_w"], p["w_down"],
                         p["norm2"], p["final_norm"], final=(l == depth - 1))
        new_c.append(c)
        new_g.append(g)
        new_f.append(f)
    return x, jnp.stack(new_c), jnp.stack(new_g), jnp.stack(new_f)


def kernel(x_prompt, x_sample, state_conv, state_gla, state_ffn_conv, w_in, w_alpha, b_alpha, conv_w, g_conv, g_gla, w_out, norm1, norm2, w_up, ffn_conv_w, w_down, final_norm):
    depth = w_in.shape[0]
    gate_cols = w_in[:, :, C_R:C_R + GATE_RANK]
    w_in_r = jnp.concatenate(
        [w_in[:, :, :C_R], w_in[:, :, C_R + GATE_RANK:],
         jnp.pad(gate_cols, ((0, 0), (0, 0), (0, LANES - GATE_RANK)))], axis=-1).astype(BF16)
    w_alpha_p = jnp.pad(w_alpha, ((0, 0), (0, LANES - GATE_RANK), (0, 0))).astype(BF16)
    p = {
        "w_in": w_in_r, "w_alpha": w_alpha_p,
        "b_alpha": b_alpha.reshape(depth, 1, D_QK),
        "conv_w": conv_w, "g_conv": g_conv.reshape(depth, 1, D_CONV),
        "g_gla": g_gla.reshape(depth, 1, HEAD_V),
        "w_out": w_out.astype(BF16),
        "norm1": norm1.reshape(depth, 1, D_MODEL), "norm2": norm2.reshape(depth, 1, D_MODEL),
        "w_up": w_up.astype(BF16), "ffn_conv_w": ffn_conv_w, "w_down": w_down.astype(BF16),
        "final_norm": final_norm.reshape(1, D_MODEL),
    }
    bp = x_prompt.shape[0]
    dt = x_prompt.dtype
    zc = jnp.zeros((depth, bp) + state_conv.shape[2:], dt)
    zg = jnp.zeros((depth, bp) + state_gla.shape[2:], dt)
    zf = jnp.zeros((depth, bp) + state_ffn_conv.shape[2:], dt)
    y_prompt, pc, pg, pf = _trunk(x_prompt, zc, zg, zf, p)
    y_sample, sc, sg, sf = _trunk(x_sample, state_conv, state_gla, state_ffn_conv, p)
    return (y_prompt, y_sample, pc, pg, pf, sc, sg, sf)
```

```python
import functools

import jax
import jax.numpy as jnp
from jax import lax
from jax.experimental import pallas as pl
from jax.experimental.pallas import tpu as pltpu

F32 = jnp.float32
BF16 = jnp.bfloat16

D_MODEL = 1024
D_CONV = 512
N_HEADS = 4
HEAD_K = 64
HEAD_V = 128
D_QK = N_HEADS * HEAD_K
D_GLA = N_HEADS * HEAD_V
GATE_RANK = 16
GATE_TAU = 16.0
D_FF = 2816
CHUNK = 64
EPS = 1e-6

LANES = 128
SUBLANES = 8
C_BG, C_CG, C_H = 0, D_CONV, 2 * D_CONV
C_Q = 3 * D_CONV
C_K = C_Q + D_QK
C_V = C_K + D_QK
C_R = C_V + D_GLA
C_GR = C_R + D_GLA
D_IN_PAD = C_GR + LANES

FFN_COL_BLOCK = 256
FFN_TIME_BLOCK = 1024
MIXER_TIME_BLOCK = 1024
MIXER_ROW_BLOCK = 512
MIXER_PROJ_COLS = 512
VMEM_LIMIT_BYTES = 56 * 1024 * 1024


def _rms_scale(v):
    return lax.rsqrt(jnp.mean(v * v, axis=-1, keepdims=True) + EPS)


def _log_sigmoid(v):
    return jnp.minimum(v, 0.0) - jnp.log1p(jnp.exp(-jnp.abs(v)))


def _causal_taps(u, prev, w):
    t = u.shape[0]
    row = lax.broadcasted_iota(jnp.int32, (t, 1), 0)
    s1 = jnp.where(row == 0, prev[1:2], pltpu.roll(u, 1, 0))
    s2 = jnp.where(row == 0, prev[0:1], jnp.where(row == 1, prev[1:2], pltpu.roll(u, 2, 0)))
    return s2 * w[0:1] + s1 * w[1:2] + u * w[2:3]


def _causal_taps_interleaved(u, prev, w):
    t = u.shape[0]
    sub = lax.broadcasted_iota(jnp.int32, (SUBLANES, 1), 0)
    before1 = jnp.where(sub == 0, prev[1:2], pltpu.roll(u[t - SUBLANES:t], 1, 0))
    before2 = jnp.where(sub == 0, prev[0:1],
                        pltpu.roll(u[t - 2 * SUBLANES:t - SUBLANES], 1, 0))
    s1 = jnp.concatenate([before1, u[:t - SUBLANES]], axis=0)
    s2 = jnp.concatenate([before2, before1, u[:t - 2 * SUBLANES]], axis=0)
    return s2 * w[0:1] + s1 * w[1:2] + u * w[2:3]


def _half_lane_mask(rows, head):
    lane = lax.broadcasted_iota(jnp.int32, (rows, LANES), 1)
    return (lane // HEAD_K) == (head % 2)


def _block_rows(ref, r0, n, t):
    if t >= n:
        return ref[r0 // t, r0 % t:r0 % t + n, :]
    return ref[r0 // t:(r0 + n) // t].reshape(n, ref.shape[-1])


def _store_block_rows(ref, r0, n, t, val):
    if t >= n:
        ref[r0 // t, r0 % t:r0 % t + n, :] = val
    else:
        ref[r0 // t:(r0 + n) // t] = val.reshape(n // t, t, ref.shape[-1])


def _mixer_kernel(x_ref, conv0_ref, gla0_ref, w_in_ref, w_alpha_ref, b_alpha_ref, conv_w_ref,
                  g_conv_ref, g_gla_ref, w_out_ref, norm_ref,
                  y_ref, conv_ref, gla_ref,
                  xn_ref, z_ref, qt_ref, kt_ref, kd_ref, v_ref, o_ref, ycat_ref,
                  *, bt, t, rb):
    m = bt * t
    ncs = rb // CHUNK
    s = pl.program_id(1)

    @pl.when(s == 0)
    def _():
        conv_ref[...] = conv0_ref[...]
        gla_ref[...] = gla0_ref[...]

    causal = (lax.broadcasted_iota(jnp.int32, (CHUNK, D_QK), 1) % CHUNK
              <= lax.broadcasted_iota(jnp.int32, (CHUNK, D_QK), 0))
    zero_kk = jnp.zeros((CHUNK, LANES), BF16)
    zero_vv = jnp.zeros((CHUNK, HEAD_V), BF16)
    zero_kv = jnp.zeros((HEAD_K, HEAD_V), BF16)

    def projection_pieces(r0):
        slot = (r0 // rb) % 2
        x = _block_rows(x_ref, r0, rb, t)
        xn_ref[slot] = (x * _rms_scale(x) * norm_ref[...]).astype(BF16)

        def piece(c0):
            c1 = min(c0 + MIXER_PROJ_COLS, D_IN_PAD)
            z_ref[slot, :, c0:c1] = jnp.dot(xn_ref[slot], w_in_ref[:, c0:c1],
                                            preferred_element_type=F32)
        return [functools.partial(piece, c0) for c0 in range(0, D_IN_PAD, MIXER_PROJ_COLS)]

    def run(pieces, n):
        for _ in range(min(n, len(pieces))):
            pieces.pop(0)()

    first = projection_pieces(0)
    run(first, len(first))
    pending_out = None
    for r0 in range(0, m, rb):
        slot = (r0 // rb) % 2
        z = z_ref.at[slot]
        qt_s, kt_s, kd_s, v_s = qt_ref.at[slot], kt_ref.at[slot], kd_ref.at[slot], v_ref.at[slot]
        o_s, ycat_s = o_ref.at[slot], ycat_ref.at[slot]
        ahead = projection_pieces(r0 + rb) if r0 + rb < m else []

        gate = jnp.dot(z[:, C_GR:C_GR + LANES].astype(BF16), w_alpha_ref[...],
                       preferred_element_type=F32) + b_alpha_ref[...]
        if pending_out is not None:
            pending_out()
        run(ahead, 2)

        bcum = _log_sigmoid(gate) * (1.0 / GATE_TAU)
        pos = lax.broadcasted_iota(jnp.int32, (rb, 1), 0) % CHUNK
        shift = 1
        while shift < CHUNK:
            bcum = bcum + jnp.where(pos >= shift, pltpu.roll(bcum, shift, 0), 0.0)
            shift *= 2
        b3 = bcum.reshape(ncs, CHUNK, D_QK)
        b_last = b3[:, CHUNK - 1:CHUNK, :]
        q = z[:, C_Q:C_Q + D_QK] * (HEAD_K ** -0.5)
        k = z[:, C_K:C_K + D_QK]
        qt_s[...] = (q * jnp.exp(bcum)).astype(BF16)
        kt_s[...] = (k * jnp.exp(-bcum)).astype(BF16)
        kd_s[...] = (k.reshape(ncs, CHUNK, D_QK)
                     * jnp.exp(b_last - b3)).reshape(rb, D_QK).astype(BF16)
        dec = jnp.exp(b_last)
        v_s[...] = z[:, C_V:C_V + D_GLA].astype(BF16)

        for c in range(ncs):
            b = (r0 + c * CHUNK) // t
            crow = slice(c * CHUNK, (c + 1) * CHUNK)
            qt = qt_s[crow, :]
            kt = kt_s[crow, :]
            kd = kd_s[crow, :]
            v = v_s[crow, :]
            k_bd = jnp.concatenate([
                jnp.concatenate([
                    jnp.where(_half_lane_mask(CHUNK, h), kt[:, g * LANES:(g + 1) * LANES], 0.0)
                    if g == h // 2 else zero_kk for g in range(D_QK // LANES)], axis=1)
                for h in range(N_HEADS)], axis=0)
            att = lax.dot_general(qt, k_bd, (((1,), (1,)), ((), ())), preferred_element_type=F32)
            ds = lax.dot_general(kd, v, (((0,), (0,)), ((), ())), preferred_element_type=F32)
            run(ahead, 1)
            att = jnp.where(causal, att, 0.0)
            v_bd = jnp.concatenate([
                jnp.concatenate([v[:, h * HEAD_V:(h + 1) * HEAD_V] if g == h else zero_vv
                                 for g in range(N_HEADS)], axis=1)
                for h in range(N_HEADS)], axis=0)
            s_blocks = [gla_ref[b, h] for h in range(N_HEADS)]
            s_bd = jnp.concatenate([
                jnp.concatenate([s_blocks[h].astype(BF16) if g == h else zero_kv
                                 for g in range(N_HEADS)], axis=1)
                for h in range(N_HEADS)], axis=0)
            o = jnp.dot(att.astype(BF16), v_bd, preferred_element_type=F32)
            o = o + jnp.dot(qt, s_bd, preferred_element_type=F32)
            o_s[crow, :] = o
            dec_col = jnp.broadcast_to(dec[c], (LANES, D_QK)).T
            for h in range(N_HEADS):
                gla_ref[b, h] = (s_blocks[h] * dec_col[h * HEAD_K:(h + 1) * HEAD_K, :]
                                 + ds[h * HEAD_K:(h + 1) * HEAD_K, h * HEAD_V:(h + 1) * HEAD_V])

        piece = min(t, rb)
        for p0 in range(0, rb, piece):
            b = (r0 + p0) // t
            prow = slice(p0, p0 + piece)
            ch = z[prow, C_CG:C_CG + D_CONV] * z[prow, C_H:C_H + D_CONV]
            cv = _causal_taps(ch, conv_ref[b], conv_w_ref[...])
            conv_ref[b] = ch[piece - 2:piece]
            yc = z[prow, C_BG:C_BG + D_CONV] * cv
            ycat_s[prow, 0:D_CONV] = (yc * _rms_scale(yc) * g_conv_ref[...]).astype(BF16)

        run(ahead, len(ahead))
        for h in range(N_HEADS):
            o = o_s[:, h * HEAD_V:(h + 1) * HEAD_V]
            r = z[:, C_R + h * HEAD_V:C_R + (h + 1) * HEAD_V]
            o = o * _rms_scale(o) * g_gla_ref[...]
            ycat_s[:, D_CONV + h * HEAD_V:D_CONV + (h + 1) * HEAD_V] = (
                o * (r * jax.nn.sigmoid(r))).astype(BF16)

        def project_out(r0=r0, ycat_s=ycat_s):
            y = jnp.dot(ycat_s[...], w_out_ref[...], preferred_element_type=F32)
            _store_block_rows(y_ref, r0, rb, t, _block_rows(x_ref, r0, rb, t) + y)
        pending_out = project_out
    pending_out()


def _ffn_kernel(x_ref, st0_ref, w_up_ref, conv_w_ref, w_down_ref, norm_ref, fnorm_ref,
                y_ref, st_ref, perm_ref, xn_ref, hid_ref, *, bt, t, final):
    m = bt * t
    s = pl.program_id(1)
    g = t // SUBLANES
    n_slabs = D_MODEL // LANES

    @pl.when(s == 0)
    def _():
        st_ref[...] = st0_ref[...]

    for b in range(bt):
        x = x_ref[b]
        xn = x * _rms_scale(x) * norm_ref[...]
        for a in range(SUBLANES):
            for l in range(n_slabs):
                perm_ref[l, pl.ds(b * t + a, g, stride=SUBLANES), :] = (
                    xn[a * g:(a + 1) * g, l * LANES:(l + 1) * LANES])
    xn_ref[...] = jnp.concatenate([perm_ref[l] for l in range(n_slabs)], axis=1).astype(BF16)

    for j in range(D_FF // FFN_COL_BLOCK):
        halves = []
        for base in (0, D_FF):
            cols = slice(base + j * FFN_COL_BLOCK, base + (j + 1) * FFN_COL_BLOCK)
            u = jnp.dot(xn_ref[...], w_up_ref[:, cols], preferred_element_type=F32)
            w = conv_w_ref[:, cols]
            parts = []
            for b in range(bt):
                ub = u[b * t:(b + 1) * t]
                parts.append(_causal_taps_interleaved(ub, st_ref[b, :, cols], w))
                st_ref[b, 0:1, cols] = ub[t - SUBLANES - 1:t - SUBLANES]
                st_ref[b, 1:2, cols] = ub[t - 1:t]
            halves.append(parts[0] if bt == 1 else jnp.concatenate(parts, axis=0))
        a, gate = halves
        hid_ref[:, j * FFN_COL_BLOCK:(j + 1) * FFN_COL_BLOCK] = (
            gate * jax.nn.sigmoid(gate) * a).astype(BF16)

    y = jnp.dot(hid_ref[...], w_down_ref[...], preferred_element_type=F32)
    for l in range(n_slabs):
        perm_ref[l] = y[:, l * LANES:(l + 1) * LANES]
    for b in range(bt):
        y = jnp.concatenate([
            jnp.concatenate([perm_ref[l, pl.ds(b * t + a, g, stride=SUBLANES), :]
                             for l in range(n_slabs)], axis=1)
            for a in range(SUBLANES)], axis=0)
        y = x_ref[b] + y
        if final:
            y = y * _rms_scale(y) * fnorm_ref[...]
        y_ref[b] = y


def _const_spec(shape):
    return pl.BlockSpec(shape, lambda b, s: (0,) * len(shape), pipeline_mode=pl.Buffered(1))


def _layer_spec(stacked, layer):
    rest = stacked.shape[1:]
    return pl.BlockSpec((None,) + rest, lambda b, s: (layer,) + (0,) * len(rest),
                        pipeline_mode=pl.Buffered(1))


def _state_spec(stacked, layer, bt):
    rest = stacked.shape[2:]
    return pl.BlockSpec((None, bt) + rest, lambda b, s: (layer, b) + (0,) * len(rest))


def _block_sizes(batch, length, time_block):
    t = min(length, time_block)
    bt = max(1, min(batch, time_block // t))
    assert length % t == 0 and batch % bt == 0 and t % CHUNK == 0
    return bt, t


def _mixer_call(x, layer, conv0, gla0, w_in, w_alpha, b_alpha, conv_w, g_conv, g_gla, w_out, norm):
    batch, length, _ = x.shape
    bt, t = _block_sizes(batch, length, MIXER_TIME_BLOCK)
    m = bt * t
    rb = min(m, MIXER_ROW_BLOCK)
    assert m % rb == 0 and (t % rb == 0 or rb % t == 0)
    x_spec = pl.BlockSpec((bt, t, D_MODEL), lambda b, s: (b, s, 0))
    conv_spec = pl.BlockSpec((bt, 2, D_CONV), lambda b, s: (b, 0, 0))
    gla_spec = pl.BlockSpec((bt, N_HEADS, HEAD_K, HEAD_V), lambda b, s: (b, 0, 0, 0))
    return pl.pallas_call(
        functools.partial(_mixer_kernel, bt=bt, t=t, rb=rb),
        grid=(batch // bt, length // t),
        in_specs=[x_spec, _state_spec(conv0, layer, bt), _state_spec(gla0, layer, bt)]
        + [_layer_spec(w, layer)
           for w in (w_in, w_alpha, b_alpha, conv_w, g_conv, g_gla, w_out, norm)],
        out_specs=[x_spec, conv_spec, gla_spec],
        out_shape=[jax.ShapeDtypeStruct(x.shape, F32),
                   jax.ShapeDtypeStruct(conv0.shape[1:], F32),
                   jax.ShapeDtypeStruct(gla0.shape[1:], F32)],
        scratch_shapes=[
            pltpu.VMEM((2, rb, D_MODEL), BF16),
            pltpu.VMEM((2, rb, D_IN_PAD), F32),
            pltpu.VMEM((2, rb, D_QK), BF16),
            pltpu.VMEM((2, rb, D_QK), BF16),
            pltpu.VMEM((2, rb, D_QK), BF16),
            pltpu.VMEM((2, rb, D_GLA), BF16),
            pltpu.VMEM((2, rb, D_GLA), F32),
            pltpu.VMEM((2, rb, D_MODEL), BF16),
        ],
        compiler_params=pltpu.CompilerParams(
            dimension_semantics=("arbitrary", "arbitrary"),
            vmem_limit_bytes=VMEM_LIMIT_BYTES),
        name="mixer",
    )(x, conv0, gla0, w_in, w_alpha, b_alpha, conv_w, g_conv, g_gla, w_out, norm)


def _ffn_call(x, layer, st0, w_up, conv_w, w_down, norm, fnorm, final):
    batch, length, _ = x.shape
    bt, t = _block_sizes(batch, length, FFN_TIME_BLOCK)
    m = bt * t
    x_spec = pl.BlockSpec((bt, t, D_MODEL), lambda b, s: (b, s, 0))
    st_spec = pl.BlockSpec((bt, 2, 2 * D_FF), lambda b, s: (b, 0, 0))
    return pl.pallas_call(
        functools.partial(_ffn_kernel, bt=bt, t=t, final=final),
        grid=(batch // bt, length // t),
        in_specs=[x_spec, _state_spec(st0, layer, bt)]
        + [_layer_spec(w, layer) for w in (w_up, conv_w, w_down, norm)]
        + [_const_spec(fnorm.shape)],
        out_specs=[x_spec, st_spec],
        out_shape=[jax.ShapeDtypeStruct(x.shape, F32),
                   jax.ShapeDtypeStruct(st0.shape[1:], F32)],
        scratch_shapes=[pltpu.VMEM((D_MODEL // LANES, m, LANES), F32),
                        pltpu.VMEM((m, D_MODEL), BF16),
                        pltpu.VMEM((m, D_FF), BF16)],
        compiler_params=pltpu.CompilerParams(
            dimension_semantics=("arbitrary", "arbitrary"),
            vmem_limit_bytes=VMEM_LIMIT_BYTES),
        name="ffn",
    )(x, st0, w_up, conv_w, w_down, norm, fnorm)


def _trunk(x, conv_s, gla_s, ffn_s, p):
    depth = conv_s.shape[0]
    new_c, new_g, new_f = [], [], []
    for l in range(depth):
        x, c, g = _mixer_call(x, l, conv_s, gla_s, p["w_in"], p["w_alpha"], p["b_alpha"],
                              p["conv_w"], p["g_conv"], p["g_gla"], p["w_out"], p["norm1"])
        x, f = _ffn_call(x, l, ffn_s, p["w_up"], p["ffn_conv_w"], p["w_down"],
                         p["norm2"], p["final_norm"], final=(l == depth - 1))
        new_c.append(c)
        new_g.append(g)
        new_f.append(f)
    return x, jnp.stack(new_c), jnp.stack(new_g), jnp.stack(new_f)


def kernel(x_prompt, x_sample, state_conv, state_gla, state_ffn_conv, w_in, w_alpha, b_alpha, conv_w, g_conv, g_gla, w_out, norm1, norm2, w_up, ffn_conv_w, w_down, final_norm):
    depth = w_in.shape[0]
    gate_cols = w_in[:, :, C_R:C_R + GATE_RANK]
    w_in_r = jnp.concatenate(
        [w_in[:, :, :C_R], w_in[:, :, C_R + GATE_RANK:],
         jnp.pad(gate_cols, ((0, 0), (0, 0), (0, LANES - GATE_RANK)))], axis=-1).astype(BF16)
    w_alpha_p = jnp.pad(w_alpha, ((0, 0), (0, LANES - GATE_RANK), (0, 0))).astype(BF16)
    p = {
        "w_in": w_in_r, "w_alpha": w_alpha_p,
        "b_alpha": b_alpha.reshape(depth, 1, D_QK),
        "conv_w": conv_w, "g_conv": g_conv.reshape(depth, 1, D_CONV),
        "g_gla": g_gla.reshape(depth, 1, HEAD_V),
        "w_out": w_out.astype(BF16),
        "norm1": norm1.reshape(depth, 1, D_MODEL), "norm2": norm2.reshape(depth, 1, D_MODEL),
        "w_up": w_up.astype(BF16), "ffn_conv_w": ffn_conv_w, "w_down": w_down.astype(BF16),
        "final_norm": final_norm.reshape(1, D_MODEL),
    }
    bp = x_prompt.shape[0]
    dt = x_prompt.dtype
    zc = jnp.zeros((depth, bp) + state_conv.shape[2:], dt)
    zg = jnp.zeros((depth, bp) + state_gla.shape[2:], dt)
    zf = jnp.zeros((depth, bp) + state_ffn_conv.shape[2:], dt)
    y_prompt, pc, pg, pf = _trunk(x_prompt, zc, zg, zf, p)
    y_sample, sc, sg, sf = _trunk(x_sample, state_conv, state_gla, state_ffn_conv, p)
    return (y_prompt, y_sample, pc, pg, pf, sc, sg, sf)
```

```python
import functools

import jax
import jax.numpy as jnp
from jax import lax
from jax.experimental import pallas as pl
from jax.experimental.pallas import tpu as pltpu

F32 = jnp.float32
BF16 = jnp.bfloat16

D_MODEL = 1024
D_CONV = 512
N_HEADS = 4
HEAD_K = 64
HEAD_V = 128
D_QK = N_HEADS * HEAD_K
D_GLA = N_HEADS * HEAD_V
GATE_RANK = 16
GATE_TAU = 16.0
D_FF = 2816
CHUNK = 64
EPS = 1e-6

LANES = 128
SUBLANES = 8
C_BG, C_CG, C_H = 0, D_CONV, 2 * D_CONV
C_Q = 3 * D_CONV
C_K = C_Q + D_QK
C_V = C_K + D_QK
C_R = C_V + D_GLA
C_GR = C_R + D_GLA
D_IN_PAD = C_GR + LANES

FFN_COL_BLOCK = 256
FFN_TIME_BLOCK = 1024
MIXER_ROW_BLOCK = 512
MIXER_PROJ_COLS = 512
MIXER_CHUNKS_PER_PIECE = 2
VMEM_LIMIT_BYTES = 56 * 1024 * 1024
MIXER_VMEM_LIMIT_BYTES = 62 * 1024 * 1024


def _rms_scale(v):
    return lax.rsqrt(jnp.mean(v * v, axis=-1, keepdims=True) + EPS)


def _log_sigmoid(v):
    return jnp.minimum(v, 0.0) - jnp.log1p(jnp.exp(-jnp.abs(v)))


def _causal_taps(u, prev, w):
    t = u.shape[0]
    row = lax.broadcasted_iota(jnp.int32, (t, 1), 0)
    s1 = jnp.where(row == 0, prev[1:2], pltpu.roll(u, 1, 0))
    s2 = jnp.where(row == 0, prev[0:1], jnp.where(row == 1, prev[1:2], pltpu.roll(u, 2, 0)))
    return s2 * w[0:1] + s1 * w[1:2] + u * w[2:3]


def _causal_taps_interleaved(u, prev, w):
    t = u.shape[0]
    sub = lax.broadcasted_iota(jnp.int32, (SUBLANES, 1), 0)
    before1 = jnp.where(sub == 0, prev[1:2], pltpu.roll(u[t - SUBLANES:t], 1, 0))
    before2 = jnp.where(sub == 0, prev[0:1],
                        pltpu.roll(u[t - 2 * SUBLANES:t - SUBLANES], 1, 0))
    s1 = jnp.concatenate([before1, u[:t - SUBLANES]], axis=0)
    s2 = jnp.concatenate([before2, before1, u[:t - 2 * SUBLANES]], axis=0)
    return s2 * w[0:1] + s1 * w[1:2] + u * w[2:3]


def _half_lane_mask(rows, head):
    lane = lax.broadcasted_iota(jnp.int32, (rows, LANES), 1)
    return (lane // HEAD_K) == (head % 2)


def _block_rows(ref, r0, n, t):
    if t >= n:
        return ref[r0 // t, r0 % t:r0 % t + n, :]
    return ref[r0 // t:(r0 + n) // t].reshape(n, ref.shape[-1])


def _store_block_rows(ref, r0, n, t, val):
    if t >= n:
        ref[r0 // t, r0 % t:r0 % t + n, :] = val
    else:
        ref[r0 // t:(r0 + n) // t] = val.reshape(n // t, t, ref.shape[-1])


def _mixer_kernel(x_ref, xnext_ref, conv0_ref, gla0_ref, w_in_ref, w_alpha_ref, b_alpha_ref,
                  conv_w_ref, g_conv_ref, g_gla_ref, w_out_ref, norm_ref,
                  y_ref, conv_ref, gla_ref,
                  xn_ref, z_ref, qt_ref, kt_ref, kd_ref, v_ref, o_ref, ycat_ref,
                  *, t, rb, steps_per_group):
    ncs = rb // CHUNK
    j = pl.program_id(0)

    @pl.when(j % steps_per_group == 0)
    def _():
        conv_ref[...] = conv0_ref[...]
        gla_ref[...] = gla0_ref[...]

    causal = (lax.broadcasted_iota(jnp.int32, (CHUNK, D_QK), 1) % CHUNK
              <= lax.broadcasted_iota(jnp.int32, (CHUNK, D_QK), 0))
    zero_kk = jnp.zeros((CHUNK, LANES), BF16)
    zero_vv = jnp.zeros((CHUNK, HEAD_V), BF16)
    zero_kv = jnp.zeros((HEAD_K, HEAD_V), BF16)

    def normalize(x, slot):
        xn_ref[slot] = (x * _rms_scale(x) * norm_ref[...]).astype(BF16)

    def projection_pieces(slot):
        def piece(c0):
            c1 = min(c0 + MIXER_PROJ_COLS, D_IN_PAD)
            z_ref[slot, :, c0:c1] = jnp.dot(xn_ref[slot], w_in_ref[:, c0:c1],
                                            preferred_element_type=F32)
        return [functools.partial(piece, c0) for c0 in range(0, D_IN_PAD, MIXER_PROJ_COLS)]

    def run(pieces, n):
        for _ in range(min(n, len(pieces))):
            pieces.pop(0)()

    @pl.when(j == 0)
    def _():
        normalize(_block_rows(x_ref, 0, rb, t), 0)
        first = projection_pieces(0)
        run(first, len(first))
        normalize(_block_rows(x_ref, rb, rb, t), 1)

    for slot, r0 in enumerate((0, rb)):
        z = z_ref.at[slot]
        qt_s, kt_s, kd_s, v_s, o_s, ycat_s = qt_ref, kt_ref, kd_ref, v_ref, o_ref, ycat_ref
        ahead = projection_pieces(1 - slot)
        normalize(_block_rows(xnext_ref, r0, rb, t), slot)

        gate = jnp.dot(z[:, C_GR:C_GR + LANES].astype(BF16), w_alpha_ref[...],
                       preferred_element_type=F32) + b_alpha_ref[...]
        run(ahead, 2)

        bcum = _log_sigmoid(gate) * (1.0 / GATE_TAU)
        pos = lax.broadcasted_iota(jnp.int32, (rb, 1), 0) % CHUNK
        shift = 1
        while shift < CHUNK:
            bcum = bcum + jnp.where(pos >= shift, pltpu.roll(bcum, shift, 0), 0.0)
            shift *= 2
        b3 = bcum.reshape(ncs, CHUNK, D_QK)
        b_last = b3[:, CHUNK - 1:CHUNK, :]
        q = z[:, C_Q:C_Q + D_QK] * (HEAD_K ** -0.5)
        k = z[:, C_K:C_K + D_QK]
        qt_s[...] = (q * jnp.exp(bcum)).astype(BF16)
        kt_s[...] = (k * jnp.exp(-bcum)).astype(BF16)
        kd_s[...] = (k.reshape(ncs, CHUNK, D_QK)
                     * jnp.exp(b_last - b3)).reshape(rb, D_QK).astype(BF16)
        dec = jnp.exp(b_last)
        v_s[...] = z[:, C_V:C_V + D_GLA].astype(BF16)

        def chunk_products(c):
            crow = slice(c * CHUNK, (c + 1) * CHUNK)
            qt = qt_s[crow, :]
            kt = kt_s[crow, :]
            kd = kd_s[crow, :]
            v = v_s[crow, :]
            k_bd = jnp.concatenate([
                jnp.concatenate([
                    jnp.where(_half_lane_mask(CHUNK, h), kt[:, g * LANES:(g + 1) * LANES], 0.0)
                    if g == h // 2 else zero_kk for g in range(D_QK // LANES)], axis=1)
                for h in range(N_HEADS)], axis=0)
            att = lax.dot_general(qt, k_bd, (((1,), (1,)), ((), ())), preferred_element_type=F32)
            ds = lax.dot_general(kd, v, (((0,), (0,)), ((), ())), preferred_element_type=F32)
            return qt, v, att, ds

        products = chunk_products(0)
        for c in range(ncs):
            b = (r0 + c * CHUNK) // t
            crow = slice(c * CHUNK, (c + 1) * CHUNK)
            qt, v, att, ds = products
            if c + 1 < ncs:
                products = chunk_products(c + 1)
            if c % MIXER_CHUNKS_PER_PIECE == MIXER_CHUNKS_PER_PIECE - 1:
                run(ahead, 1)
            att = jnp.where(causal, att, 0.0)
            v_bd = jnp.concatenate([
                jnp.concatenate([v[:, h * HEAD_V:(h + 1) * HEAD_V] if g == h else zero_vv
                                 for g in range(N_HEADS)], axis=1)
                for h in range(N_HEADS)], axis=0)
            s_blocks = [gla_ref[b, h] for h in range(N_HEADS)]
            s_bd = jnp.concatenate([
                jnp.concatenate([s_blocks[h].astype(BF16) if g == h else zero_kv
                                 for g in range(N_HEADS)], axis=1)
                for h in range(N_HEADS)], axis=0)
            o = jnp.dot(att.astype(BF16), v_bd, preferred_element_type=F32)
            o = o + jnp.dot(qt, s_bd, preferred_element_type=F32)
            o_s[crow, :] = o
            dec_col = jnp.broadcast_to(dec[c], (LANES, D_QK)).T
            for h in range(N_HEADS):
                gla_ref[b, h] = (s_blocks[h] * dec_col[h * HEAD_K:(h + 1) * HEAD_K, :]
                                 + ds[h * HEAD_K:(h + 1) * HEAD_K, h * HEAD_V:(h + 1) * HEAD_V])

        piece = min(t, rb)
        for p0 in range(0, rb, piece):
            b = (r0 + p0) // t
            prow = slice(p0, p0 + piece)
            ch = z[prow, C_CG:C_CG + D_CONV] * z[prow, C_H:C_H + D_CONV]
            cv = _causal_taps(ch, conv_ref[b], conv_w_ref[...])
            conv_ref[b] = ch[piece - 2:piece]
            yc = z[prow, C_BG:C_BG + D_CONV] * cv
            ycat_s[prow, 0:D_CONV] = (yc * _rms_scale(yc) * g_conv_ref[...]).astype(BF16)

        run(ahead, len(ahead))
        for h in range(N_HEADS):
            o = o_s[:, h * HEAD_V:(h + 1) * HEAD_V]
            r = z[:, C_R + h * HEAD_V:C_R + (h + 1) * HEAD_V]
            o = o * _rms_scale(o) * g_gla_ref[...]
            ycat_s[:, D_CONV + h * HEAD_V:D_CONV + (h + 1) * HEAD_V] = (
                o * (r * jax.nn.sigmoid(r))).astype(BF16)
        y = jnp.dot(ycat_s[...], w_out_ref[...], preferred_element_type=F32)
        _store_block_rows(y_ref, r0, rb, t, _block_rows(x_ref, r0, rb, t) + y)


def _ffn_kernel(x_ref, st0_ref, w_up_ref, conv_w_ref, w_down_ref, norm_ref, fnorm_ref,
                y_ref, st_ref, perm_ref, xn_ref, hid_ref, *, bt, t, final):
    m = bt * t
    s = pl.program_id(1)
    g = t // SUBLANES
    n_slabs = D_MODEL // LANES

    @pl.when(s == 0)
    def _():
        st_ref[...] = st0_ref[...]

    for b in range(bt):
        x = x_ref[b]
        xn = x * _rms_scale(x) * norm_ref[...]
        for a in range(SUBLANES):
            for l in range(n_slabs):
                perm_ref[l, pl.ds(b * t + a, g, stride=SUBLANES), :] = (
                    xn[a * g:(a + 1) * g, l * LANES:(l + 1) * LANES])
    xn_ref[...] = jnp.concatenate([perm_ref[l] for l in range(n_slabs)], axis=1).astype(BF16)

    for j in range(D_FF // FFN_COL_BLOCK):
        halves = []
        for base in (0, D_FF):
            cols = slice(base + j * FFN_COL_BLOCK, base + (j + 1) * FFN_COL_BLOCK)
            u = jnp.dot(xn_ref[...], w_up_ref[:, cols], preferred_element_type=F32)
            w = conv_w_ref[:, cols]
            parts = []
            for b in range(bt):
                ub = u[b * t:(b + 1) * t]
                parts.append(_causal_taps_interleaved(ub, st_ref[b, :, cols], w))
                st_ref[b, 0:1, cols] = ub[t - SUBLANES - 1:t - SUBLANES]
                st_ref[b, 1:2, cols] = ub[t - 1:t]
            halves.append(parts[0] if bt == 1 else jnp.concatenate(parts, axis=0))
        a, gate = halves
        hid_ref[:, j * FFN_COL_BLOCK:(j + 1) * FFN_COL_BLOCK] = (
            gate * jax.nn.sigmoid(gate) * a).astype(BF16)

    y = jnp.dot(hid_ref[...], w_down_ref[...], preferred_element_type=F32)
    for l in range(n_slabs):
        perm_ref[l] = y[:, l * LANES:(l + 1) * LANES]
    for b in range(bt):
        y = jnp.concatenate([
            jnp.concatenate([perm_ref[l, pl.ds(b * t + a, g, stride=SUBLANES), :]
                             for l in range(n_slabs)], axis=1)
            for a in range(SUBLANES)], axis=0)
        y = x_ref[b] + y
        if final:
            y = y * _rms_scale(y) * fnorm_ref[...]
        y_ref[b] = y


def _const_spec(shape):
    return pl.BlockSpec(shape, lambda b, s: (0,) * len(shape), pipeline_mode=pl.Buffered(1))


def _layer_spec(stacked, layer):
    rest = stacked.shape[1:]
    return pl.BlockSpec((None,) + rest, lambda b, s: (layer,) + (0,) * len(rest),
                        pipeline_mode=pl.Buffered(1))


def _state_spec(stacked, layer, bt):
    rest = stacked.shape[2:]
    return pl.BlockSpec((None, bt) + rest, lambda b, s: (layer, b) + (0,) * len(rest))


def _block_sizes(batch, length, time_block):
    t = min(length, time_block)
    bt = max(1, min(batch, time_block // t))
    assert length % t == 0 and batch % bt == 0 and t % CHUNK == 0
    return bt, t


def _mixer_call(x, layer, conv0, gla0, w_in, w_alpha, b_alpha, conv_w, g_conv, g_gla, w_out, norm):
    batch, length, _ = x.shape
    rb = MIXER_ROW_BLOCK if length >= 2 * MIXER_ROW_BLOCK else MIXER_ROW_BLOCK // 2
    t = min(length, 2 * rb)
    bt = 2 * rb // t
    assert (2 * rb) % t == 0 and length % t == 0 and batch % bt == 0
    assert rb % min(t, rb) == 0 and min(t, rb) % CHUNK == 0
    tiles_per_seq = length // t
    n_tiles = (batch // bt) * tiles_per_seq

    def tile_index(j):
        return (j // tiles_per_seq, j % tiles_per_seq, 0)

    def next_tile_index(j):
        return tile_index(jnp.minimum(j + 1, n_tiles - 1))

    def layer_spec(stacked):
        rest = stacked.shape[1:]
        return pl.BlockSpec((None,) + rest, lambda j: (layer,) + (0,) * len(rest),
                            pipeline_mode=pl.Buffered(1))

    def state_in_spec(stacked):
        rest = stacked.shape[2:]
        return pl.BlockSpec((None, bt) + rest,
                            lambda j: (layer, j // tiles_per_seq) + (0,) * len(rest),
                            pipeline_mode=pl.Buffered(1))

    x_spec = pl.BlockSpec((bt, t, D_MODEL), tile_index)
    xnext_spec = pl.BlockSpec((bt, t, D_MODEL), next_tile_index)
    conv_spec = pl.BlockSpec((bt, 2, D_CONV), lambda j: (j // tiles_per_seq, 0, 0))
    gla_spec = pl.BlockSpec((bt, N_HEADS, HEAD_K, HEAD_V),
                            lambda j: (j // tiles_per_seq, 0, 0, 0))
    return pl.pallas_call(
        functools.partial(_mixer_kernel, t=t, rb=rb, steps_per_group=tiles_per_seq),
        grid=(n_tiles,),
        in_specs=[x_spec, xnext_spec, state_in_spec(conv0), state_in_spec(gla0)]
        + [layer_spec(w) for w in (w_in, w_alpha, b_alpha, conv_w, g_conv, g_gla, w_out, norm)],
        out_specs=[x_spec, conv_spec, gla_spec],
        out_shape=[jax.ShapeDtypeStruct(x.shape, F32),
                   jax.ShapeDtypeStruct(conv0.shape[1:], F32),
                   jax.ShapeDtypeStruct(gla0.shape[1:], F32)],
        scratch_shapes=[
            pltpu.VMEM((2, rb, D_MODEL), BF16),
            pltpu.VMEM((2, rb, D_IN_PAD), F32),
            pltpu.VMEM((rb, D_QK), BF16),
            pltpu.VMEM((rb, D_QK), BF16),
            pltpu.VMEM((rb, D_QK), BF16),
            pltpu.VMEM((rb, D_GLA), BF16),
            pltpu.VMEM((rb, D_GLA), F32),
            pltpu.VMEM((rb, D_MODEL), BF16),
        ],
        compiler_params=pltpu.CompilerParams(
            dimension_semantics=("arbitrary",),
            vmem_limit_bytes=MIXER_VMEM_LIMIT_BYTES),
        name="mixer",
    )(x, x, conv0, gla0, w_in, w_alpha, b_alpha, conv_w, g_conv, g_gla, w_out, norm)


def _ffn_call(x, layer, st0, w_up, conv_w, w_down, norm, fnorm, final):
    batch, length, _ = x.shape
    bt, t = _block_sizes(batch, length, FFN_TIME_BLOCK)
    m = bt * t
    x_spec = pl.BlockSpec((bt, t, D_MODEL), lambda b, s: (b, s, 0))
    st_spec = pl.BlockSpec((bt, 2, 2 * D_FF), lambda b, s: (b, 0, 0))
    return pl.pallas_call(
        functools.partial(_ffn_kernel, bt=bt, t=t, final=final),
        grid=(batch // bt, length // t),
        in_specs=[x_spec, _state_spec(st0, layer, bt)]
        + [_layer_spec(w, layer) for w in (w_up, conv_w, w_down, norm)]
        + [_const_spec(fnorm.shape)],
        out_specs=[x_spec, st_spec],
        out_shape=[jax.ShapeDtypeStruct(x.shape, F32),
                   jax.ShapeDtypeStruct(st0.shape[1:], F32)],
        scratch_shapes=[pltpu.VMEM((D_MODEL // LANES, m, LANES), F32),
                        pltpu.VMEM((m, D_MODEL), BF16),
                        pltpu.VMEM((m, D_FF), BF16)],
        compiler_params=pltpu.CompilerParams(
            dimension_semantics=("arbitrary", "arbitrary"),
            vmem_limit_bytes=VMEM_LIMIT_BYTES),
        name="ffn",
    )(x, st0, w_up, conv_w, w_down, norm, fnorm)


def _trunk(x, conv_s, gla_s, ffn_s, p):
    depth = conv_s.shape[0]
    new_c, new_g, new_f = [], [], []
    for l in range(depth):
        x, c, g = _mixer_call(x, l, conv_s, gla_s, p["w_in"], p["w_alpha"], p["b_alpha"],
                              p["conv_w"], p["g_conv"], p["g_gla"], p["w_out"], p["norm1"])
        x, f = _ffn_call(x, l, ffn_s, p["w_up"], p["ffn_conv_w"], p["w_down"],
                         p["norm2"], p["final_norm"], final=(l == depth - 1))
        new_c.append(c)
        new_g.append(g)
        new_f.append(f)
    return x, jnp.stack(new_c), jnp.stack(new_g), jnp.stack(new_f)


def kernel(x_prompt, x_sample, state_conv, state_gla, state_ffn_conv, w_in, w_alpha, b_alpha, conv_w, g_conv, g_gla, w_out, norm1, norm2, w_up, ffn_conv_w, w_down, final_norm):
    depth = w_in.shape[0]
    gate_cols = w_in[:, :, C_R:C_R + GATE_RANK]
    w_in_r = jnp.concatenate(
        [w_in[:, :, :C_R], w_in[:, :, C_R + GATE_RANK:],
         jnp.pad(gate_cols, ((0, 0), (0, 0), (0, LANES - GATE_RANK)))], axis=-1).astype(BF16)
    w_alpha_p = jnp.pad(w_alpha, ((0, 0), (0, LANES - GATE_RANK), (0, 0))).astype(BF16)
    p = {
        "w_in": w_in_r, "w_alpha": w_alpha_p,
        "b_alpha": b_alpha.reshape(depth, 1, D_QK),
        "conv_w": conv_w, "g_conv": g_conv.reshape(depth, 1, D_CONV),
        "g_gla": g_gla.reshape(depth, 1, HEAD_V),
        "w_out": w_out.astype(BF16),
        "norm1": norm1.reshape(depth, 1, D_MODEL), "norm2": norm2.reshape(depth, 1, D_MODEL),
        "w_up": w_up.astype(BF16), "ffn_conv_w": ffn_conv_w, "w_down": w_down.astype(BF16),
        "final_norm": final_norm.reshape(1, D_MODEL),
    }
    bp = x_prompt.shape[0]
    dt = x_prompt.dtype
    zc = jnp.zeros((depth, bp) + state_conv.shape[2:], dt)
    zg = jnp.zeros((depth, bp) + state_gla.shape[2:], dt)
    zf = jnp.zeros((depth, bp) + state_ffn_conv.shape[2:], dt)
    y_prompt, pc, pg, pf = _trunk(x_prompt, zc, zg, zf, p)
    y_sample, sc, sg, sf = _trunk(x_sample, state_conv, state_gla, state_ffn_conv, p)
    return (y_prompt, y_sample, pc, pg, pf, sc, sg, sf)
```

```python
import functools

import jax
import jax.numpy as jnp
from jax import lax
from jax.experimental import pallas as pl
from jax.experimental.pallas import tpu as pltpu

F32 = jnp.float32
BF16 = jnp.bfloat16

D_MODEL = 1024
D_CONV = 512
N_HEADS = 4
HEAD_K = 64
HEAD_V = 128
D_QK = N_HEADS * HEAD_K
D_GLA = N_HEADS * HEAD_V
GATE_RANK = 16
GATE_TAU = 16.0
D_FF = 2816
CHUNK = 64
EPS = 1e-6

LANES = 128
SUBLANES = 8
C_BG, C_CG, C_H = 0, D_CONV, 2 * D_CONV
C_Q = 3 * D_CONV
C_K = C_Q + D_QK
C_V = C_K + D_QK
C_R = C_V + D_GLA
C_GR = C_R + D_GLA
D_IN_PAD = C_GR + LANES

FFN_COL_BLOCK = 256
FFN_TIME_BLOCK = 1024
MIXER_TIME_BLOCK = 512
MIXER_ROW_BLOCK = 256
MIXER_PROJ_COLS = 512
VMEM_LIMIT_BYTES = 56 * 1024 * 1024


def _rms_scale(v):
    return lax.rsqrt(jnp.mean(v * v, axis=-1, keepdims=True) + EPS)


def _log_sigmoid(v):
    return jnp.minimum(v, 0.0) - jnp.log1p(jnp.exp(-jnp.abs(v)))


def _causal_taps(u, prev, w):
    t = u.shape[0]
    row = lax.broadcasted_iota(jnp.int32, (t, 1), 0)
    s1 = jnp.where(row == 0, prev[1:2], pltpu.roll(u, 1, 0))
    s2 = jnp.where(row == 0, prev[0:1], jnp.where(row == 1, prev[1:2], pltpu.roll(u, 2, 0)))
    return s2 * w[0:1] + s1 * w[1:2] + u * w[2:3]


def _causal_taps_interleaved(u, prev, w):
    t = u.shape[0]
    sub = lax.broadcasted_iota(jnp.int32, (SUBLANES, 1), 0)
    before1 = jnp.where(sub == 0, prev[1:2], pltpu.roll(u[t - SUBLANES:t], 1, 0))
    before2 = jnp.where(sub == 0, prev[0:1],
                        pltpu.roll(u[t - 2 * SUBLANES:t - SUBLANES], 1, 0))
    s1 = jnp.concatenate([before1, u[:t - SUBLANES]], axis=0)
    s2 = jnp.concatenate([before2, before1, u[:t - 2 * SUBLANES]], axis=0)
    return s2 * w[0:1] + s1 * w[1:2] + u * w[2:3]


def _half_lane_mask(rows, head):
    lane = lax.broadcasted_iota(jnp.int32, (rows, LANES), 1)
    return (lane // HEAD_K) == (head % 2)


def _block_rows(ref, r0, n, t):
    if t >= n:
        return ref[r0 // t, r0 % t:r0 % t + n, :]
    return ref[r0 // t:(r0 + n) // t].reshape(n, ref.shape[-1])


def _store_block_rows(ref, r0, n, t, val):
    if t >= n:
        ref[r0 // t, r0 % t:r0 % t + n, :] = val
    else:
        ref[r0 // t:(r0 + n) // t] = val.reshape(n // t, t, ref.shape[-1])


def _mixer_kernel(x_ref, conv0_ref, gla0_ref, w_in_ref, w_alpha_ref, b_alpha_ref, conv_w_ref,
                  g_conv_ref, g_gla_ref, w_out_ref, norm_ref,
                  y_ref, conv_ref, gla_ref,
                  xn_ref, z_ref, qt_ref, kt_ref, kd_ref, v_ref, o_ref, ycat_ref,
                  *, bt, t, rb):
    m = bt * t
    ncs = rb // CHUNK
    s = pl.program_id(1)

    @pl.when(s == 0)
    def _():
        conv_ref[...] = conv0_ref[...]
        gla_ref[...] = gla0_ref[...]

    causal = (lax.broadcasted_iota(jnp.int32, (CHUNK, D_QK), 1) % CHUNK
              <= lax.broadcasted_iota(jnp.int32, (CHUNK, D_QK), 0))
    zero_kk = jnp.zeros((CHUNK, LANES), BF16)
    zero_vv = jnp.zeros((CHUNK, HEAD_V), BF16)
    zero_kv = jnp.zeros((HEAD_K, HEAD_V), BF16)

    def projection_pieces(r0):
        slot = (r0 // rb) % 2
        x = _block_rows(x_ref, r0, rb, t)
        xn_ref[slot] = (x * _rms_scale(x) * norm_ref[...]).astype(BF16)

        def piece(c0):
            c1 = min(c0 + MIXER_PROJ_COLS, D_IN_PAD)
            z_ref[slot, :, c0:c1] = jnp.dot(xn_ref[slot], w_in_ref[:, c0:c1],
                                            preferred_element_type=F32)
        return [functools.partial(piece, c0) for c0 in range(0, D_IN_PAD, MIXER_PROJ_COLS)]

    def run(pieces, n):
        for _ in range(min(n, len(pieces))):
            pieces.pop(0)()

    first = projection_pieces(0)
    run(first, len(first))
    pending_out = None
    for r0 in range(0, m, rb):
        slot = (r0 // rb) % 2
        z = z_ref.at[slot]
        qt_s, kt_s, kd_s, v_s = qt_ref.at[slot], kt_ref.at[slot], kd_ref.at[slot], v_ref.at[slot]
        o_s, ycat_s = o_ref.at[slot], ycat_ref.at[slot]
        ahead = projection_pieces(r0 + rb) if r0 + rb < m else []

        gate = jnp.dot(z[:, C_GR:C_GR + LANES].astype(BF16), w_alpha_ref[...],
                       preferred_element_type=F32) + b_alpha_ref[...]
        if pending_out is not None:
            pending_out()
        run(ahead, 2)

        bcum = _log_sigmoid(gate) * (1.0 / GATE_TAU)
        pos = lax.broadcasted_iota(jnp.int32, (rb, 1), 0) % CHUNK
        shift = 1
        while shift < CHUNK:
            bcum = bcum + jnp.where(pos >= shift, pltpu.roll(bcum, shift, 0), 0.0)
            shift *= 2
        b3 = bcum.reshape(ncs, CHUNK, D_QK)
        b_last = b3[:, CHUNK - 1:CHUNK, :]
        q = z[:, C_Q:C_Q + D_QK] * (HEAD_K ** -0.5)
        k = z[:, C_K:C_K + D_QK]
        qt_s[...] = (q * jnp.exp(bcum)).astype(BF16)
        kt_s[...] = (k * jnp.exp(-bcum)).astype(BF16)
        kd_s[...] = (k.reshape(ncs, CHUNK, D_QK)
                     * jnp.exp(b_last - b3)).reshape(rb, D_QK).astype(BF16)
        dec = jnp.exp(b_last)
        v_s[...] = z[:, C_V:C_V + D_GLA].astype(BF16)

        def chunk_products(c):
            crow = slice(c * CHUNK, (c + 1) * CHUNK)
            qt = qt_s[crow, :]
            kt = kt_s[crow, :]
            kd = kd_s[crow, :]
            v = v_s[crow, :]
            k_bd = jnp.concatenate([
                jnp.concatenate([
                    jnp.where(_half_lane_mask(CHUNK, h), kt[:, g * LANES:(g + 1) * LANES], 0.0)
                    if g == h // 2 else zero_kk for g in range(D_QK // LANES)], axis=1)
                for h in range(N_HEADS)], axis=0)
            att = lax.dot_general(qt, k_bd, (((1,), (1,)), ((), ())), preferred_element_type=F32)
            ds = lax.dot_general(kd, v, (((0,), (0,)), ((), ())), preferred_element_type=F32)
            return qt, v, att, ds

        products = chunk_products(0)
        for c in range(ncs):
            b = (r0 + c * CHUNK) // t
            crow = slice(c * CHUNK, (c + 1) * CHUNK)
            qt, v, att, ds = products
            if c + 1 < ncs:
                products = chunk_products(c + 1)
            run(ahead, 1)
            att = jnp.where(causal, att, 0.0)
            v_bd = jnp.concatenate([
                jnp.concatenate([v[:, h * HEAD_V:(h + 1) * HEAD_V] if g == h else zero_vv
                                 for g in range(N_HEADS)], axis=1)
                for h in range(N_HEADS)], axis=0)
            s_blocks = [gla_ref[b, h] for h in range(N_HEADS)]
            s_bd = jnp.concatenate([
                jnp.concatenate([s_blocks[h].astype(BF16) if g == h else zero_kv
                                 for g in range(N_HEADS)], axis=1)
                for h in range(N_HEADS)], axis=0)
            o = jnp.dot(att.astype(BF16), v_bd, preferred_element_type=F32)
            o = o + jnp.dot(qt, s_bd, preferred_element_type=F32)
            o_s[crow, :] = o
            dec_col = jnp.broadcast_to(dec[c], (LANES, D_QK)).T
            for h in range(N_HEADS):
                gla_ref[b, h] = (s_blocks[h] * dec_col[h * HEAD_K:(h + 1) * HEAD_K, :]
                                 + ds[h * HEAD_K:(h + 1) * HEAD_K, h * HEAD_V:(h + 1) * HEAD_V])

        piece = min(t, rb)
        for p0 in range(0, rb, piece):
            b = (r0 + p0) // t
            prow = slice(p0, p0 + piece)
            ch = z[prow, C_CG:C_CG + D_CONV] * z[prow, C_H:C_H + D_CONV]
            cv = _causal_taps(ch, conv_ref[b], conv_w_ref[...])
            conv_ref[b] = ch[piece - 2:piece]
            yc = z[prow, C_BG:C_BG + D_CONV] * cv
            ycat_s[prow, 0:D_CONV] = (yc * _rms_scale(yc) * g_conv_ref[...]).astype(BF16)

        run(ahead, len(ahead))
        for h in range(N_HEADS):
            o = o_s[:, h * HEAD_V:(h + 1) * HEAD_V]
            r = z[:, C_R + h * HEAD_V:C_R + (h + 1) * HEAD_V]
            o = o * _rms_scale(o) * g_gla_ref[...]
            ycat_s[:, D_CONV + h * HEAD_V:D_CONV + (h + 1) * HEAD_V] = (
                o * (r * jax.nn.sigmoid(r))).astype(BF16)

        def project_out(r0=r0, ycat_s=ycat_s):
            y = jnp.dot(ycat_s[...], w_out_ref[...], preferred_element_type=F32)
            _store_block_rows(y_ref, r0, rb, t, _block_rows(x_ref, r0, rb, t) + y)
        pending_out = project_out
    pending_out()


def _ffn_kernel(x_ref, st0_ref, w_up_ref, conv_w_ref, w_down_ref, norm_ref, fnorm_ref,
                y_ref, st_ref, perm_ref, xn_ref, hid_ref, *, bt, t, final):
    m = bt * t
    s = pl.program_id(1)
    g = t // SUBLANES
    n_slabs = D_MODEL // LANES

    @pl.when(s == 0)
    def _():
        st_ref[...] = st0_ref[...]

    for b in range(bt):
        x = x_ref[b]
        xn = x * _rms_scale(x) * norm_ref[...]
        for a in range(SUBLANES):
            for l in range(n_slabs):
                perm_ref[l, pl.ds(b * t + a, g, stride=SUBLANES), :] = (
                    xn[a * g:(a + 1) * g, l * LANES:(l + 1) * LANES])
    xn_ref[...] = jnp.concatenate([perm_ref[l] for l in range(n_slabs)], axis=1).astype(BF16)

    for j in range(D_FF // FFN_COL_BLOCK):
        halves = []
        for base in (0, D_FF):
            cols = slice(base + j * FFN_COL_BLOCK, base + (j + 1) * FFN_COL_BLOCK)
            u = jnp.dot(xn_ref[...], w_up_ref[:, cols], preferred_element_type=F32)
            w = conv_w_ref[:, cols]
            parts = []
            for b in range(bt):
                ub = u[b * t:(b + 1) * t]
                parts.append(_causal_taps_interleaved(ub, st_ref[b, :, cols], w))
                st_ref[b, 0:1, cols] = ub[t - SUBLANES - 1:t - SUBLANES]
                st_ref[b, 1:2, cols] = ub[t - 1:t]
            halves.append(parts[0] if bt == 1 else jnp.concatenate(parts, axis=0))
        a, gate = halves
        hid_ref[:, j * FFN_COL_BLOCK:(j + 1) * FFN_COL_BLOCK] = (
            gate * jax.nn.sigmoid(gate) * a).astype(BF16)

    y = jnp.dot(hid_ref[...], w_down_ref[...], preferred_element_type=F32)
    for l in range(n_slabs):
        perm_ref[l] = y[:, l * LANES:(l + 1) * LANES]
    for b in range(bt):
        y = jnp.concatenate([
            jnp.concatenate([perm_ref[l, pl.ds(b * t + a, g, stride=SUBLANES), :]
                             for l in range(n_slabs)], axis=1)
            for a in range(SUBLANES)], axis=0)
        y = x_ref[b] + y
        if final:
            y = y * _rms_scale(y) * fnorm_ref[...]
        y_ref[b] = y


def _const_spec(shape):
    return pl.BlockSpec(shape, lambda b, s: (0,) * len(shape), pipeline_mode=pl.Buffered(1))


def _layer_spec(stacked, layer):
    rest = stacked.shape[1:]
    return pl.BlockSpec((None,) + rest, lambda b, s: (layer,) + (0,) * len(rest),
                        pipeline_mode=pl.Buffered(1))


def _state_spec(stacked, layer, bt):
    rest = stacked.shape[2:]
    return pl.BlockSpec((None, bt) + rest, lambda b, s: (layer, b) + (0,) * len(rest))


def _block_sizes(batch, length, time_block):
    t = min(length, time_block)
    bt = max(1, min(batch, time_block // t))
    assert length % t == 0 and batch % bt == 0 and t % CHUNK == 0
    return bt, t


def _mixer_call(x, layer, conv0, gla0, w_in, w_alpha, b_alpha, conv_w, g_conv, g_gla, w_out, norm):
    batch, length, _ = x.shape
    bt, t = _block_sizes(batch, length, MIXER_TIME_BLOCK)
    m = bt * t
    rb = min(m, MIXER_ROW_BLOCK)
    assert m % rb == 0 and (t % rb == 0 or rb % t == 0)
    x_spec = pl.BlockSpec((bt, t, D_MODEL), lambda b, s: (b, s, 0))
    conv_spec = pl.BlockSpec((bt, 2, D_CONV), lambda b, s: (b, 0, 0))
    gla_spec = pl.BlockSpec((bt, N_HEADS, HEAD_K, HEAD_V), lambda b, s: (b, 0, 0, 0))
    return pl.pallas_call(
        functools.partial(_mixer_kernel, bt=bt, t=t, rb=rb),
        grid=(batch // bt, length // t),
        in_specs=[x_spec, _state_spec(conv0, layer, bt), _state_spec(gla0, layer, bt)]
        + [_layer_spec(w, layer)
           for w in (w_in, w_alpha, b_alpha, conv_w, g_conv, g_gla, w_out, norm)],
        out_specs=[x_spec, conv_spec, gla_spec],
        out_shape=[jax.ShapeDtypeStruct(x.shape, F32),
                   jax.ShapeDtypeStruct(conv0.shape[1:], F32),
                   jax.ShapeDtypeStruct(gla0.shape[1:], F32)],
        scratch_shapes=[
            pltpu.VMEM((2, rb, D_MODEL), BF16),
            pltpu.VMEM((2, rb, D_IN_PAD), F32),
            pltpu.VMEM((2, rb, D_QK), BF16),
            pltpu.VMEM((2, rb, D_QK), BF16),
            pltpu.VMEM((2, rb, D_QK), BF16),
            pltpu.VMEM((2, rb, D_GLA), BF16),
            pltpu.VMEM((2, rb, D_GLA), F32),
            pltpu.VMEM((2, rb, D_MODEL), BF16),
        ],
        compiler_params=pltpu.CompilerParams(
            dimension_semantics=("arbitrary", "arbitrary"),
            vmem_limit_bytes=VMEM_LIMIT_BYTES),
        name="mixer",
    )(x, conv0, gla0, w_in, w_alpha, b_alpha, conv_w, g_conv, g_gla, w_out, norm)


def _ffn_call(x, layer, st0, w_up, conv_w, w_down, norm, fnorm, final):
    batch, length, _ = x.shape
    bt, t = _block_sizes(batch, length, FFN_TIME_BLOCK)
    m = bt * t
    x_spec = pl.BlockSpec((bt, t, D_MODEL), lambda b, s: (b, s, 0))
    st_spec = pl.BlockSpec((bt, 2, 2 * D_FF), lambda b, s: (b, 0, 0))
    return pl.pallas_call(
        functools.partial(_ffn_kernel, bt=bt, t=t, final=final),
        grid=(batch // bt, length // t),
        in_specs=[x_spec, _state_spec(st0, layer, bt)]
        + [_layer_spec(w, layer) for w in (w_up, conv_w, w_down, norm)]
        + [_const_spec(fnorm.shape)],
        out_specs=[x_spec, st_spec],
        out_shape=[jax.ShapeDtypeStruct(x.shape, F32),
                   jax.ShapeDtypeStruct(st0.shape[1:], F32)],
        scratch_shapes=[pltpu.VMEM((D_MODEL // LANES, m, LANES), F32),
                        pltpu.VMEM((m, D_MODEL), BF16),
                        pltpu.VMEM((m, D_FF), BF16)],
        compiler_params=pltpu.CompilerParams(
            dimension_semantics=("arbitrary", "arbitrary"),
            vmem_limit_bytes=VMEM_LIMIT_BYTES),
        name="ffn",
    )(x, st0, w_up, conv_w, w_down, norm, fnorm)


def _trunk(x, conv_s, gla_s, ffn_s, p):
    depth = conv_s.shape[0]
    new_c, new_g, new_f = [], [], []
    for l in range(depth):
        x, c, g = _mixer_call(x, l, conv_s, gla_s, p["w_in"], p["w_alpha"], p["b_alpha"],
                              p["conv_w"], p["g_conv"], p["g_gla"], p["w_out"], p["norm1"])
        x, f = _ffn_call(x, l, ffn_s, p["w_up"], p["ffn_conv_w"], p["w_down"],
                         p["norm2"], p["final_norm"], final=(l == depth - 1))
        new_c.append(c)
        new_g.append(g)
        new_f.append(f)
    return x, jnp.stack(new_c), jnp.stack(new_g), jnp.stack(new_f)


def kernel(x_prompt, x_sample, state_conv, state_gla, state_ffn_conv, w_in, w_alpha, b_alpha, conv_w, g_conv, g_gla, w_out, norm1, norm2, w_up, ffn_conv_w, w_down, final_norm):
    depth = w_in.shape[0]
    gate_cols = w_in[:, :, C_R:C_R + GATE_RANK]
    w_in_r = jnp.concatenate(
        [w_in[:, :, :C_R], w_in[:, :, C_R + GATE_RANK:],
         jnp.pad(gate_cols, ((0, 0), (0, 0), (0, LANES - GATE_RANK)))], axis=-1).astype(BF16)
    w_alpha_p = jnp.pad(w_alpha, ((0, 0), (0, LANES - GATE_RANK), (0, 0))).astype(BF16)
    p = {
        "w_in": w_in_r, "w_alpha": w_alpha_p,
        "b_alpha": b_alpha.reshape(depth, 1, D_QK),
        "conv_w": conv_w, "g_conv": g_conv.reshape(depth, 1, D_CONV),
        "g_gla": g_gla.reshape(depth, 1, HEAD_V),
        "w_out": w_out.astype(BF16),
        "norm1": norm1.reshape(depth, 1, D_MODEL), "norm2": norm2.reshape(depth, 1, D_MODEL),
        "w_up": w_up.astype(BF16), "ffn_conv_w": ffn_conv_w, "w_down": w_down.astype(BF16),
        "final_norm": final_norm.reshape(1, D_MODEL),
    }
    bp = x_prompt.shape[0]
    dt = x_prompt.dtype
    zc = jnp.zeros((depth, bp) + state_conv.shape[2:], dt)
    zg = jnp.zeros((depth, bp) + state_gla.shape[2:], dt)
    zf = jnp.zeros((depth, bp) + state_ffn_conv.shape[2:], dt)
    y_prompt, pc, pg, pf = _trunk(x_prompt, zc, zg, zf, p)
    y_sample, sc, sg, sf = _trunk(x_sample, state_conv, state_gla, state_ffn_conv, p)
    return (y_prompt, y_sample, pc, pg, pf, sc, sg, sf)
```

```python
import functools

import jax
import jax.numpy as jnp
from jax import lax
from jax.experimental import pallas as pl
from jax.experimental.pallas import tpu as pltpu

F32 = jnp.float32
BF16 = jnp.bfloat16

D_MODEL = 1024
D_CONV = 512
N_HEADS = 4
HEAD_K = 64
HEAD_V = 128
D_QK = N_HEADS * HEAD_K
D_GLA = N_HEADS * HEAD_V
GATE_RANK = 16
GATE_TAU = 16.0
D_FF = 2816
CHUNK = 64
EPS = 1e-6

LANES = 128
SUBLANES = 8
C_BG, C_CG, C_H = 0, D_CONV, 2 * D_CONV
C_Q = 3 * D_CONV
C_K = C_Q + D_QK
C_V = C_K + D_QK
C_R = C_V + D_GLA
C_GR = C_R + D_GLA
D_IN_PAD = C_GR + LANES

FFN_COL_BLOCK = 256
FFN_TIME_BLOCK = 1024
MIXER_TIME_BLOCK = 512
MIXER_ROW_BLOCK = 256
MIXER_PROJ_COLS = 512
VMEM_LIMIT_BYTES = 56 * 1024 * 1024


def _rms_scale(v):
    return lax.rsqrt(jnp.mean(v * v, axis=-1, keepdims=True) + EPS)


def _log_sigmoid(v):
    return jnp.minimum(v, 0.0) - jnp.log1p(jnp.exp(-jnp.abs(v)))


def _causal_taps(u, prev, w):
    t = u.shape[0]
    row = lax.broadcasted_iota(jnp.int32, (t, 1), 0)
    s1 = jnp.where(row == 0, prev[1:2], pltpu.roll(u, 1, 0))
    s2 = jnp.where(row == 0, prev[0:1], jnp.where(row == 1, prev[1:2], pltpu.roll(u, 2, 0)))
    return s2 * w[0:1] + s1 * w[1:2] + u * w[2:3]


def _causal_taps_interleaved(u, prev, w):
    t = u.shape[0]
    sub = lax.broadcasted_iota(jnp.int32, (SUBLANES, 1), 0)
    before1 = jnp.where(sub == 0, prev[1:2], pltpu.roll(u[t - SUBLANES:t], 1, 0))
    before2 = jnp.where(sub == 0, prev[0:1],
                        pltpu.roll(u[t - 2 * SUBLANES:t - SUBLANES], 1, 0))
    s1 = jnp.concatenate([before1, u[:t - SUBLANES]], axis=0)
    s2 = jnp.concatenate([before2, before1, u[:t - 2 * SUBLANES]], axis=0)
    return s2 * w[0:1] + s1 * w[1:2] + u * w[2:3]


def _half_lane_mask(rows, head):
    lane = lax.broadcasted_iota(jnp.int32, (rows, LANES), 1)
    return (lane // HEAD_K) == (head % 2)


def _block_rows(ref, r0, n, t):
    if t >= n:
        return ref[r0 // t, r0 % t:r0 % t + n, :]
    return ref[r0 // t:(r0 + n) // t].reshape(n, ref.shape[-1])


def _store_block_rows(ref, r0, n, t, val):
    if t >= n:
        ref[r0 // t, r0 % t:r0 % t + n, :] = val
    else:
        ref[r0 // t:(r0 + n) // t] = val.reshape(n // t, t, ref.shape[-1])


def _mixer_kernel(x_ref, conv0_ref, gla0_ref, w_in_ref, w_alpha_ref, b_alpha_ref, conv_w_ref,
                  g_conv_ref, g_gla_ref, w_out_ref, norm_ref,
                  y_ref, conv_ref, gla_ref,
                  xn_ref, z_ref, qt_ref, kt_ref, kd_ref, v_ref, o_ref, ycat_ref,
                  *, bt, t, rb):
    m = bt * t
    ncs = rb // CHUNK
    s = pl.program_id(1)

    @pl.when(s == 0)
    def _():
        conv_ref[...] = conv0_ref[...]
        gla_ref[...] = gla0_ref[...]

    causal = (lax.broadcasted_iota(jnp.int32, (CHUNK, D_QK), 1) % CHUNK
              <= lax.broadcasted_iota(jnp.int32, (CHUNK, D_QK), 0))
    zero_kk = jnp.zeros((CHUNK, LANES), BF16)
    zero_vv = jnp.zeros((CHUNK, HEAD_V), BF16)
    zero_kv = jnp.zeros((HEAD_K, HEAD_V), BF16)

    def projection_pieces(r0):
        slot = (r0 // rb) % 2
        x = _block_rows(x_ref, r0, rb, t)
        xn_ref[slot] = (x * _rms_scale(x) * norm_ref[...]).astype(BF16)

        def piece(c0):
            c1 = min(c0 + MIXER_PROJ_COLS, D_IN_PAD)
            z_ref[slot, :, c0:c1] = jnp.dot(xn_ref[slot], w_in_ref[:, c0:c1],
                                            preferred_element_type=F32)
        return [functools.partial(piece, c0) for c0 in range(0, D_IN_PAD, MIXER_PROJ_COLS)]

    def run(pieces, n):
        for _ in range(min(n, len(pieces))):
            pieces.pop(0)()

    first = projection_pieces(0)
    run(first, len(first))
    pending_out = None
    for r0 in range(0, m, rb):
        slot = (r0 // rb) % 2
        z = z_ref.at[slot]
        qt_s, kt_s, kd_s, v_s = qt_ref.at[slot], kt_ref.at[slot], kd_ref.at[slot], v_ref.at[slot]
        o_s, ycat_s = o_ref.at[slot], ycat_ref.at[slot]
        ahead = projection_pieces(r0 + rb) if r0 + rb < m else []

        gate = jnp.dot(z[:, C_GR:C_GR + LANES].astype(BF16), w_alpha_ref[...],
                       preferred_element_type=F32) + b_alpha_ref[...]
        if pending_out is not None:
            pending_out()
        run(ahead, 2)

        bcum = _log_sigmoid(gate) * (1.0 / GATE_TAU)
        pos = lax.broadcasted_iota(jnp.int32, (rb, 1), 0) % CHUNK
        shift = 1
        while shift < CHUNK:
            bcum = bcum + jnp.where(pos >= shift, pltpu.roll(bcum, shift, 0), 0.0)
            shift *= 2
        b3 = bcum.reshape(ncs, CHUNK, D_QK)
        b_last = b3[:, CHUNK - 1:CHUNK, :]
        q = z[:, C_Q:C_Q + D_QK] * (HEAD_K ** -0.5)
        k = z[:, C_K:C_K + D_QK]
        qt_s[...] = (q * jnp.exp(bcum)).astype(BF16)
        kt_s[...] = (k * jnp.exp(-bcum)).astype(BF16)
        kd_s[...] = (k.reshape(ncs, CHUNK, D_QK)
                     * jnp.exp(b_last - b3)).reshape(rb, D_QK).astype(BF16)
        dec = jnp.exp(b_last)
        v_s[...] = z[:, C_V:C_V + D_GLA].astype(BF16)

        def chunk_products(c):
            crow = slice(c * CHUNK, (c + 1) * CHUNK)
            qt = qt_s[crow, :]
            kt = kt_s[crow, :]
            kd = kd_s[crow, :]
            v = v_s[crow, :]
            k_bd = jnp.concatenate([
                jnp.concatenate([
                    jnp.where(_half_lane_mask(CHUNK, h), kt[:, g * LANES:(g + 1) * LANES], 0.0)
                    if g == h // 2 else zero_kk for g in range(D_QK // LANES)], axis=1)
                for h in range(N_HEADS)], axis=0)
            att = lax.dot_general(qt, k_bd, (((1,), (1,)), ((), ())), preferred_element_type=F32)
            ds = [lax.dot_general(kd[:, p * LANES:(p + 1) * LANES],
                                  v[:, 2 * p * HEAD_V:2 * (p + 1) * HEAD_V],
                                  (((0,), (0,)), ((), ())), preferred_element_type=F32)
                  for p in range(N_HEADS // 2)]
            return qt, v, att, ds

        products = chunk_products(0)
        for c in range(ncs):
            b = (r0 + c * CHUNK) // t
            crow = slice(c * CHUNK, (c + 1) * CHUNK)
            qt, v, att, ds = products
            if c + 1 < ncs:
                products = chunk_products(c + 1)
            run(ahead, 1)
            att = jnp.where(causal, att, 0.0)
            v_bd = jnp.concatenate([
                jnp.concatenate([v[:, h * HEAD_V:(h + 1) * HEAD_V] if g == h else zero_vv
                                 for g in range(N_HEADS)], axis=1)
                for h in range(N_HEADS)], axis=0)
            s_blocks = [gla_ref[b, h] for h in range(N_HEADS)]
            s_bd = jnp.concatenate([
                jnp.concatenate([s_blocks[h].astype(BF16) if g == h else zero_kv
                                 for g in range(N_HEADS)], axis=1)
                for h in range(N_HEADS)], axis=0)
            o = jnp.dot(att.astype(BF16), v_bd, preferred_element_type=F32)
            o = o + jnp.dot(qt, s_bd, preferred_element_type=F32)
            o_s[crow, :] = o
            dec_col = jnp.broadcast_to(dec[c], (LANES, D_QK)).T
            for h in range(N_HEADS):
                hl = h % 2
                gla_ref[b, h] = (s_blocks[h] * dec_col[h * HEAD_K:(h + 1) * HEAD_K, :]
                                 + ds[h // 2][hl * HEAD_K:(hl + 1) * HEAD_K,
                                              hl * HEAD_V:(hl + 1) * HEAD_V])

        piece = min(t, rb)
        for p0 in range(0, rb, piece):
            b = (r0 + p0) // t
            prow = slice(p0, p0 + piece)
            ch = z[prow, C_CG:C_CG + D_CONV] * z[prow, C_H:C_H + D_CONV]
            cv = _causal_taps(ch, conv_ref[b], conv_w_ref[...])
            conv_ref[b] = ch[piece - 2:piece]
            yc = z[prow, C_BG:C_BG + D_CONV] * cv
            ycat_s[prow, 0:D_CONV] = (yc * _rms_scale(yc) * g_conv_ref[...]).astype(BF16)

        run(ahead, len(ahead))
        for h in range(N_HEADS):
            o = o_s[:, h * HEAD_V:(h + 1) * HEAD_V]
            r = z[:, C_R + h * HEAD_V:C_R + (h + 1) * HEAD_V]
            o = o * _rms_scale(o) * g_gla_ref[...]
            ycat_s[:, D_CONV + h * HEAD_V:D_CONV + (h + 1) * HEAD_V] = (
                o * (r * jax.nn.sigmoid(r))).astype(BF16)

        def project_out(r0=r0, ycat_s=ycat_s):
            y = jnp.dot(ycat_s[...], w_out_ref[...], preferred_element_type=F32)
            _store_block_rows(y_ref, r0, rb, t, _block_rows(x_ref, r0, rb, t) + y)
        pending_out = project_out
    pending_out()


def _ffn_kernel(x_ref, st0_ref, w_up_ref, conv_w_ref, w_down_ref, norm_ref, fnorm_ref,
                y_ref, st_ref, perm_ref, xn_ref, hid_ref, *, bt, t, final):
    m = bt * t
    s = pl.program_id(1)
    g = t // SUBLANES
    n_slabs = D_MODEL // LANES

    @pl.when(s == 0)
    def _():
        st_ref[...] = st0_ref[...]

    for b in range(bt):
        x = x_ref[b]
        xn = x * _rms_scale(x) * norm_ref[...]
        for a in range(SUBLANES):
            for l in range(n_slabs):
                perm_ref[l, pl.ds(b * t + a, g, stride=SUBLANES), :] = (
                    xn[a * g:(a + 1) * g, l * LANES:(l + 1) * LANES])
    xn_ref[...] = jnp.concatenate([perm_ref[l] for l in range(n_slabs)], axis=1).astype(BF16)

    for j in range(D_FF // FFN_COL_BLOCK):
        halves = []
        for base in (0, D_FF):
            cols = slice(base + j * FFN_COL_BLOCK, base + (j + 1) * FFN_COL_BLOCK)
            u = jnp.dot(xn_ref[...], w_up_ref[:, cols], preferred_element_type=F32)
            w = conv_w_ref[:, cols]
            parts = []
            for b in range(bt):
                ub = u[b * t:(b + 1) * t]
                parts.append(_causal_taps_interleaved(ub, st_ref[b, :, cols], w))
                st_ref[b, 0:1, cols] = ub[t - SUBLANES - 1:t - SUBLANES]
                st_ref[b, 1:2, cols] = ub[t - 1:t]
            halves.append(parts[0] if bt == 1 else jnp.concatenate(parts, axis=0))
        a, gate = halves
        hid_ref[:, j * FFN_COL_BLOCK:(j + 1) * FFN_COL_BLOCK] = (
            gate * jax.nn.sigmoid(gate) * a).astype(BF16)

    y = jnp.dot(hid_ref[...], w_down_ref[...], preferred_element_type=F32)
    for l in range(n_slabs):
        perm_ref[l] = y[:, l * LANES:(l + 1) * LANES]
    for b in range(bt):
        y = jnp.concatenate([
            jnp.concatenate([perm_ref[l, pl.ds(b * t + a, g, stride=SUBLANES), :]
                             for l in range(n_slabs)], axis=1)
            for a in range(SUBLANES)], axis=0)
        y = x_ref[b] + y
        if final:
            y = y * _rms_scale(y) * fnorm_ref[...]
        y_ref[b] = y


def _const_spec(shape):
    return pl.BlockSpec(shape, lambda b, s: (0,) * len(shape), pipeline_mode=pl.Buffered(1))


def _layer_spec(stacked, layer):
    rest = stacked.shape[1:]
    return pl.BlockSpec((None,) + rest, lambda b, s: (layer,) + (0,) * len(rest),
                        pipeline_mode=pl.Buffered(1))


def _state_spec(stacked, layer, bt):
    rest = stacked.shape[2:]
    return pl.BlockSpec((None, bt) + rest, lambda b, s: (layer, b) + (0,) * len(rest))


def _block_sizes(batch, length, time_block):
    t = min(length, time_block)
    bt = max(1, min(batch, time_block // t))
    assert length % t == 0 and batch % bt == 0 and t % CHUNK == 0
    return bt, t


def _mixer_call(x, layer, conv0, gla0, w_in, w_alpha, b_alpha, conv_w, g_conv, g_gla, w_out, norm):
    batch, length, _ = x.shape
    bt, t = _block_sizes(batch, length, MIXER_TIME_BLOCK)
    m = bt * t
    rb = min(m, MIXER_ROW_BLOCK)
    assert m % rb == 0 and (t % rb == 0 or rb % t == 0)
    x_spec = pl.BlockSpec((bt, t, D_MODEL), lambda b, s: (b, s, 0))
    conv_spec = pl.BlockSpec((bt, 2, D_CONV), lambda b, s: (b, 0, 0))
    gla_spec = pl.BlockSpec((bt, N_HEADS, HEAD_K, HEAD_V), lambda b, s: (b, 0, 0, 0))
    return pl.pallas_call(
        functools.partial(_mixer_kernel, bt=bt, t=t, rb=rb),
        grid=(batch // bt, length // t),
        in_specs=[x_spec, _state_spec(conv0, layer, bt), _state_spec(gla0, layer, bt)]
        + [_layer_spec(w, layer)
           for w in (w_in, w_alpha, b_alpha, conv_w, g_conv, g_gla, w_out, norm)],
        out_specs=[x_spec, conv_spec, gla_spec],
        out_shape=[jax.ShapeDtypeStruct(x.shape, F32),
                   jax.ShapeDtypeStruct(conv0.shape[1:], F32),
                   jax.ShapeDtypeStruct(gla0.shape[1:], F32)],
        scratch_shapes=[
            pltpu.VMEM((2, rb, D_MODEL), BF16),
            pltpu.VMEM((2, rb, D_IN_PAD), F32),
            pltpu.VMEM((2, rb, D_QK), BF16),
            pltpu.VMEM((2, rb, D_QK), BF16),
            pltpu.VMEM((2, rb, D_QK), BF16),
            pltpu.VMEM((2, rb, D_GLA), BF16),
            pltpu.VMEM((2, rb, D_GLA), F32),
            pltpu.VMEM((2, rb, D_MODEL), BF16),
        ],
        compiler_params=pltpu.CompilerParams(
            dimension_semantics=("arbitrary", "arbitrary"),
            vmem_limit_bytes=VMEM_LIMIT_BYTES),
        name="mixer",
    )(x, conv0, gla0, w_in, w_alpha, b_alpha, conv_w, g_conv, g_gla, w_out, norm)


def _ffn_call(x, layer, st0, w_up, conv_w, w_down, norm, fnorm, final):
    batch, length, _ = x.shape
    bt, t = _block_sizes(batch, length, FFN_TIME_BLOCK)
    m = bt * t
    x_spec = pl.BlockSpec((bt, t, D_MODEL), lambda b, s: (b, s, 0))
    st_spec = pl.BlockSpec((bt, 2, 2 * D_FF), lambda b, s: (b, 0, 0))
    return pl.pallas_call(
        functools.partial(_ffn_kernel, bt=bt, t=t, final=final),
        grid=(batch // bt, length // t),
        in_specs=[x_spec, _state_spec(st0, layer, bt)]
        + [_layer_spec(w, layer) for w in (w_up, conv_w, w_down, norm)]
        + [_const_spec(fnorm.shape)],
        out_specs=[x_spec, st_spec],
        out_shape=[jax.ShapeDtypeStruct(x.shape, F32),
                   jax.ShapeDtypeStruct(st0.shape[1:], F32)],
        scratch_shapes=[pltpu.VMEM((D_MODEL // LANES, m, LANES), F32),
                        pltpu.VMEM((m, D_MODEL), BF16),
                        pltpu.VMEM((m, D_FF), BF16)],
        compiler_params=pltpu.CompilerParams(
            dimension_semantics=("arbitrary", "arbitrary"),
            vmem_limit_bytes=VMEM_LIMIT_BYTES),
        name="ffn",
    )(x, st0, w_up, conv_w, w_down, norm, fnorm)


def _trunk(x, conv_s, gla_s, ffn_s, p):
    depth = conv_s.shape[0]
    new_c, new_g, new_f = [], [], []
    for l in range(depth):
        x, c, g = _mixer_call(x, l, conv_s, gla_s, p["w_in"], p["w_alpha"], p["b_alpha"],
                              p["conv_w"], p["g_conv"], p["g_gla"], p["w_out"], p["norm1"])
        x, f = _ffn_call(x, l, ffn_s, p["w_up"], p["ffn_conv_w"], p["w_down"],
                         p["norm2"], p["final_norm"], final=(l == depth - 1))
        new_c.append(c)
        new_g.append(g)
        new_f.append(f)
    return x, jnp.stack(new_c), jnp.stack(new_g), jnp.stack(new_f)


def kernel(x_prompt, x_sample, state_conv, state_gla, state_ffn_conv, w_in, w_alpha, b_alpha, conv_w, g_conv, g_gla, w_out, norm1, norm2, w_up, ffn_conv_w, w_down, final_norm):
    depth = w_in.shape[0]
    gate_cols = w_in[:, :, C_R:C_R + GATE_RANK]
    w_in_r = jnp.concatenate(
        [w_in[:, :, :C_R], w_in[:, :, C_R + GATE_RANK:],
         jnp.pad(gate_cols, ((0, 0), (0, 0), (0, LANES - GATE_RANK)))], axis=-1).astype(BF16)
    w_alpha_p = jnp.pad(w_alpha, ((0, 0), (0, LANES - GATE_RANK), (0, 0))).astype(BF16)
    p = {
        "w_in": w_in_r, "w_alpha": w_alpha_p,
        "b_alpha": b_alpha.reshape(depth, 1, D_QK),
        "conv_w": conv_w, "g_conv": g_conv.reshape(depth, 1, D_CONV),
        "g_gla": g_gla.reshape(depth, 1, HEAD_V),
        "w_out": w_out.astype(BF16),
        "norm1": norm1.reshape(depth, 1, D_MODEL), "norm2": norm2.reshape(depth, 1, D_MODEL),
        "w_up": w_up.astype(BF16), "ffn_conv_w": ffn_conv_w, "w_down": w_down.astype(BF16),
        "final_norm": final_norm.reshape(1, D_MODEL),
    }
    bp = x_prompt.shape[0]
    dt = x_prompt.dtype
    zc = jnp.zeros((depth, bp) + state_conv.shape[2:], dt)
    zg = jnp.zeros((depth, bp) + state_gla.shape[2:], dt)
    zf = jnp.zeros((depth, bp) + state_ffn_conv.shape[2:], dt)
    y_prompt, pc, pg, pf = _trunk(x_prompt, zc, zg, zf, p)
    y_sample, sc, sg, sf = _trunk(x_sample, state_conv, state_gla, state_ffn_conv, p)
    return (y_prompt, y_sample, pc, pg, pf, sc, sg, sf)
```

```python
import functools

import jax
import jax.numpy as jnp
from jax import lax
from jax.experimental import pallas as pl
from jax.experimental.pallas import tpu as pltpu

F32 = jnp.float32
BF16 = jnp.bfloat16

D_MODEL = 1024
D_CONV = 512
N_HEADS = 4
HEAD_K = 64
HEAD_V = 128
D_QK = N_HEADS * HEAD_K
D_GLA = N_HEADS * HEAD_V
GATE_RANK = 16
GATE_TAU = 16.0
D_FF = 2816
CHUNK = 64
EPS = 1e-6

LANES = 128
SUBLANES = 8
C_BG, C_CG, C_H = 0, D_CONV, 2 * D_CONV
C_Q = 3 * D_CONV
C_K = C_Q + D_QK
C_V = C_K + D_QK
C_R = C_V + D_GLA
C_GR = C_R + D_GLA
D_IN_PAD = C_GR + LANES

FFN_COL_BLOCK = 256
FFN_TIME_BLOCK = 1024
MIXER_TIME_BLOCK = 512
MIXER_ROW_BLOCK = 256
MIXER_PROJ_COLS = 512
VMEM_LIMIT_BYTES = 56 * 1024 * 1024


def _rms_scale(v):
    return lax.rsqrt(jnp.mean(v * v, axis=-1, keepdims=True) + EPS)


def _log_sigmoid(v):
    return jnp.minimum(v, 0.0) - jnp.log1p(jnp.exp(-jnp.abs(v)))


def _causal_taps(u, prev, w):
    t = u.shape[0]
    row = lax.broadcasted_iota(jnp.int32, (t, 1), 0)
    s1 = jnp.where(row == 0, prev[1:2], pltpu.roll(u, 1, 0))
    s2 = jnp.where(row == 0, prev[0:1], jnp.where(row == 1, prev[1:2], pltpu.roll(u, 2, 0)))
    return s2 * w[0:1] + s1 * w[1:2] + u * w[2:3]


def _causal_taps_interleaved(u, prev, w):
    t = u.shape[0]
    sub = lax.broadcasted_iota(jnp.int32, (SUBLANES, 1), 0)
    before1 = jnp.where(sub == 0, prev[1:2], pltpu.roll(u[t - SUBLANES:t], 1, 0))
    before2 = jnp.where(sub == 0, prev[0:1],
                        pltpu.roll(u[t - 2 * SUBLANES:t - SUBLANES], 1, 0))
    s1 = jnp.concatenate([before1, u[:t - SUBLANES]], axis=0)
    s2 = jnp.concatenate([before2, before1, u[:t - 2 * SUBLANES]], axis=0)
    return s2 * w[0:1] + s1 * w[1:2] + u * w[2:3]


def _half_lane_mask(rows, head):
    lane = lax.broadcasted_iota(jnp.int32, (rows, LANES), 1)
    return (lane // HEAD_K) == (head % 2)


def _block_rows(ref, r0, n, t):
    if t >= n:
        return ref[r0 // t, r0 % t:r0 % t + n, :]
    return ref[r0 // t:(r0 + n) // t].reshape(n, ref.shape[-1])


def _store_block_rows(ref, r0, n, t, val):
    if t >= n:
        ref[r0 // t, r0 % t:r0 % t + n, :] = val
    else:
        ref[r0 // t:(r0 + n) // t] = val.reshape(n // t, t, ref.shape[-1])


def _mixer_kernel(x_ref, conv0_ref, gla0_ref, w_in_ref, w_alpha_ref, b_alpha_ref, conv_w_ref,
                  g_conv_ref, g_gla_ref, w_out_ref, norm_ref,
                  y_ref, conv_ref, gla_ref,
                  xn_ref, z_ref, qt_ref, kt_ref, kd_ref, v_ref, o_ref, ycat_ref,
                  *, bt, t, rb):
    m = bt * t
    ncs = rb // CHUNK
    s = pl.program_id(1)

    @pl.when(s == 0)
    def _():
        conv_ref[...] = conv0_ref[...]
        gla_ref[...] = gla0_ref[...]

    causal = (lax.broadcasted_iota(jnp.int32, (CHUNK, D_QK), 1) % CHUNK
              <= lax.broadcasted_iota(jnp.int32, (CHUNK, D_QK), 0))
    zero_kk = jnp.zeros((CHUNK, LANES), BF16)
    zero_vv = jnp.zeros((CHUNK, HEAD_V), BF16)
    zero_kv = jnp.zeros((HEAD_K, HEAD_V), BF16)

    def projection_pieces(r0):
        slot = (r0 // rb) % 2
        x = _block_rows(x_ref, r0, rb, t)
        xn_ref[slot] = (x * _rms_scale(x) * norm_ref[...]).astype(BF16)

        def piece(c0):
            c1 = min(c0 + MIXER_PROJ_COLS, D_IN_PAD)
            z_ref[slot, :, c0:c1] = jnp.dot(xn_ref[slot], w_in_ref[:, c0:c1],
                                            preferred_element_type=F32)
        return [functools.partial(piece, c0) for c0 in range(0, D_IN_PAD, MIXER_PROJ_COLS)]

    def run(pieces, n):
        for _ in range(min(n, len(pieces))):
            pieces.pop(0)()

    first = projection_pieces(0)
    run(first, len(first))
    pending_out = None
    for r0 in range(0, m, rb):
        slot = (r0 // rb) % 2
        z = z_ref.at[slot]
        qt_s, kt_s, kd_s, v_s = qt_ref.at[slot], kt_ref.at[slot], kd_ref.at[slot], v_ref.at[slot]
        o_s, ycat_s = o_ref.at[slot], ycat_ref.at[slot]
        ahead = projection_pieces(r0 + rb) if r0 + rb < m else []

        gate = jnp.dot(z[:, C_GR:C_GR + LANES].astype(BF16), w_alpha_ref[...],
                       preferred_element_type=F32) + b_alpha_ref[...]
        if pending_out is not None:
            pending_out()
        run(ahead, 2)

        bcum = _log_sigmoid(gate) * (1.0 / GATE_TAU)
        pos = lax.broadcasted_iota(jnp.int32, (rb, 1), 0) % CHUNK
        shift = 1
        while shift < CHUNK:
            bcum = bcum + jnp.where(pos >= shift, pltpu.roll(bcum, shift, 0), 0.0)
            shift *= 2
        b3 = bcum.reshape(ncs, CHUNK, D_QK)
        b_last = b3[:, CHUNK - 1:CHUNK, :]
        q = z[:, C_Q:C_Q + D_QK] * (HEAD_K ** -0.5)
        k = z[:, C_K:C_K + D_QK]
        qt_s[...] = (q * jnp.exp(bcum)).astype(BF16)
        kt_s[...] = (k * jnp.exp(-bcum)).astype(BF16)
        kd_s[...] = (k.reshape(ncs, CHUNK, D_QK)
                     * jnp.exp(b_last - b3)).reshape(rb, D_QK).astype(BF16)
        dec = jnp.exp(b_last)
        v_s[...] = z[:, C_V:C_V + D_GLA].astype(BF16)

        def chunk_products(c):
            crow = slice(c * CHUNK, (c + 1) * CHUNK)
            qt = qt_s[crow, :]
            kt = kt_s[crow, :]
            kd = kd_s[crow, :]
            v = v_s[crow, :]
            k_bd = jnp.concatenate([
                jnp.concatenate([
                    jnp.where(_half_lane_mask(CHUNK, h), kt[:, g * LANES:(g + 1) * LANES], 0.0)
                    if g == h // 2 else zero_kk for g in range(D_QK // LANES)], axis=1)
                for h in range(N_HEADS)], axis=0)
            att = lax.dot_general(qt, k_bd, (((1,), (1,)), ((), ())), preferred_element_type=F32)
            ds = lax.dot_general(kd, v, (((0,), (0,)), ((), ())), preferred_element_type=F32)
            return qt, v, att, ds

        products = chunk_products(0)
        for c in range(ncs):
            b = (r0 + c * CHUNK) // t
            crow = slice(c * CHUNK, (c + 1) * CHUNK)
            qt, v, att, ds = products
            if c + 1 < ncs:
                products = chunk_products(c + 1)
            run(ahead, 1)
            att = jnp.where(causal, att, 0.0)
            v_bd = jnp.concatenate([
                jnp.concatenate([v[:, h * HEAD_V:(h + 1) * HEAD_V] if g == h else zero_vv
                                 for g in range(N_HEADS)], axis=1)
                for h in range(N_HEADS)], axis=0)
            s_blocks = [gla_ref[b, h] for h in range(N_HEADS)]
            s_bd = jnp.concatenate([
                jnp.concatenate([s_blocks[h].astype(BF16) if g == h else zero_kv
                                 for g in range(N_HEADS)], axis=1)
                for h in range(N_HEADS)], axis=0)
            o = jnp.dot(att.astype(BF16), v_bd, preferred_element_type=F32)
            o = o + jnp.dot(qt, s_bd, preferred_element_type=F32)
            o_s[crow, :] = o
            dec_col = jnp.broadcast_to(dec[c], (LANES, D_QK)).T
            for h in range(N_HEADS):
                gla_ref[b, h] = (s_blocks[h] * dec_col[h * HEAD_K:(h + 1) * HEAD_K, :]
                                 + ds[h * HEAD_K:(h + 1) * HEAD_K, h * HEAD_V:(h + 1) * HEAD_V])

        piece = min(t, rb)
        for p0 in range(0, rb, piece):
            b = (r0 + p0) // t
            prow = slice(p0, p0 + piece)
            ch = z[prow, C_CG:C_CG + D_CONV] * z[prow, C_H:C_H + D_CONV]
            cv = _causal_taps(ch, conv_ref[b], conv_w_ref[...])
            conv_ref[b] = ch[piece - 2:piece]
            yc = z[prow, C_BG:C_BG + D_CONV] * cv
            ycat_s[prow, 0:D_CONV] = (yc * _rms_scale(yc) * g_conv_ref[...]).astype(BF16)

        run(ahead, len(ahead))
        for h in range(N_HEADS):
            o = o_s[:, h * HEAD_V:(h + 1) * HEAD_V]
            r = z[:, C_R + h * HEAD_V:C_R + (h + 1) * HEAD_V]
            o = o * _rms_scale(o) * g_gla_ref[...]
            ycat_s[:, D_CONV + h * HEAD_V:D_CONV + (h + 1) * HEAD_V] = (
                o * (r * jax.nn.sigmoid(r))).astype(BF16)

        def project_out(r0=r0, ycat_s=ycat_s):
            y = jnp.dot(ycat_s[...], w_out_ref[...], preferred_element_type=F32)
            _store_block_rows(y_ref, r0, rb, t, _block_rows(x_ref, r0, rb, t) + y)
        pending_out = project_out
    pending_out()


def _ffn_kernel(x_ref, st0_ref, w_up_ref, conv_w_ref, w_down_ref, norm_ref, fnorm_ref,
                y_ref, st_ref, perm_ref, xn_ref, hid_ref, *, bt, t, final):
    m = bt * t
    s = pl.program_id(1)
    g = t // SUBLANES
    n_slabs = D_MODEL // LANES

    @pl.when(s == 0)
    def _():
        st_ref[...] = st0_ref[...]

    for b in range(bt):
        x = x_ref[b]
        xn = x * _rms_scale(x) * norm_ref[...]
        for a in range(SUBLANES):
            for l in range(n_slabs):
                perm_ref[l, pl.ds(b * t + a, g, stride=SUBLANES), :] = (
                    xn[a * g:(a + 1) * g, l * LANES:(l + 1) * LANES])
    xn_ref[...] = jnp.concatenate([perm_ref[l] for l in range(n_slabs)], axis=1).astype(BF16)

    for j in range(D_FF // FFN_COL_BLOCK):
        halves = []
        for base in (0, D_FF):
            cols = slice(base + j * FFN_COL_BLOCK, base + (j + 1) * FFN_COL_BLOCK)
            u = jnp.dot(xn_ref[...], w_up_ref[:, cols], preferred_element_type=F32)
            w = conv_w_ref[:, cols]
            parts = []
            for b in range(bt):
                ub = u[b * t:(b + 1) * t]
                parts.append(_causal_taps_interleaved(ub, st_ref[b, :, cols], w))
                st_ref[b, 0:1, cols] = ub[t - SUBLANES - 1:t - SUBLANES]
                st_ref[b, 1:2, cols] = ub[t - 1:t]
            halves.append(parts[0] if bt == 1 else jnp.concatenate(parts, axis=0))
        a, gate = halves
        hid_ref[:, j * FFN_COL_BLOCK:(j + 1) * FFN_COL_BLOCK] = (
            gate * jax.nn.sigmoid(gate) * a).astype(BF16)

    y = jnp.dot(hid_ref[...], w_down_ref[...], preferred_element_type=F32)
    for l in range(n_slabs):
        perm_ref[l] = y[:, l * LANES:(l + 1) * LANES]
    for b in range(bt):
        y = jnp.concatenate([
            jnp.concatenate([perm_ref[l, pl.ds(b * t + a, g, stride=SUBLANES), :]
                             for l in range(n_slabs)], axis=1)
            for a in range(SUBLANES)], axis=0)
        y = x_ref[b] + y
        if final:
            y = y * _rms_scale(y) * fnorm_ref[...]
        y_ref[b] = y


def _const_spec(shape):
    return pl.BlockSpec(shape, lambda b, s: (0,) * len(shape), pipeline_mode=pl.Buffered(1))


def _layer_spec(stacked, layer):
    rest = stacked.shape[1:]
    return pl.BlockSpec((None,) + rest, lambda b, s: (layer,) + (0,) * len(rest),
                        pipeline_mode=pl.Buffered(1))


def _state_spec(stacked, layer, bt):
    rest = stacked.shape[2:]
    return pl.BlockSpec((None, bt) + rest, lambda b, s: (layer, b) + (0,) * len(rest))


def _block_sizes(batch, length, time_block):
    t = min(length, time_block)
    bt = max(1, min(batch, time_block // t))
    assert length % t == 0 and batch % bt == 0 and t % CHUNK == 0
    return bt, t


def _mixer_call(x, layer, conv0, gla0, w_in, w_alpha, b_alpha, conv_w, g_conv, g_gla, w_out, norm):
    batch, length, _ = x.shape
    bt, t = _block_sizes(batch, length, MIXER_TIME_BLOCK)
    m = bt * t
    rb = min(m, MIXER_ROW_BLOCK)
    assert m % rb == 0 and (t % rb == 0 or rb % t == 0)
    x_spec = pl.BlockSpec((bt, t, D_MODEL), lambda b, s: (b, s, 0))
    conv_spec = pl.BlockSpec((bt, 2, D_CONV), lambda b, s: (b, 0, 0))
    gla_spec = pl.BlockSpec((bt, N_HEADS, HEAD_K, HEAD_V), lambda b, s: (b, 0, 0, 0))
    return pl.pallas_call(
        functools.partial(_mixer_kernel, bt=bt, t=t, rb=rb),
        grid=(batch // bt, length // t),
        in_specs=[x_spec, _state_spec(conv0, layer, bt), _state_spec(gla0, layer, bt)]
        + [_layer_spec(w, layer)
           for w in (w_in, w_alpha, b_alpha, conv_w, g_conv, g_gla, w_out, norm)],
        out_specs=[x_spec, conv_spec, gla_spec],
        out_shape=[jax.ShapeDtypeStruct(x.shape, F32),
                   jax.ShapeDtypeStruct(conv0.shape[1:], F32),
                   jax.ShapeDtypeStruct(gla0.shape[1:], F32)],
        scratch_shapes=[
            pltpu.VMEM((2, rb, D_MODEL), BF16),
            pltpu.VMEM((2, rb, D_IN_PAD), F32),
            pltpu.VMEM((2, rb, D_QK), BF16),
            pltpu.VMEM((2, rb, D_QK), BF16),
            pltpu.VMEM((2, rb, D_QK), BF16),
            pltpu.VMEM((2, rb, D_GLA), BF16),
            pltpu.VMEM((2, rb, D_GLA), F32),
            pltpu.VMEM((2, rb, D_MODEL), BF16),
        ],
        compiler_params=pltpu.CompilerParams(
            dimension_semantics=("arbitrary", "arbitrary"),
            vmem_limit_bytes=VMEM_LIMIT_BYTES),
        name="mixer",
    )(x, conv0, gla0, w_in, w_alpha, b_alpha, conv_w, g_conv, g_gla, w_out, norm)


def _ffn_call(x, layer, st0, w_up, conv_w, w_down, norm, fnorm, final):
    batch, length, _ = x.shape
    bt, t = _block_sizes(batch, length, FFN_TIME_BLOCK)
    m = bt * t
    x_spec = pl.BlockSpec((bt, t, D_MODEL), lambda b, s: (b, s, 0))
    st_spec = pl.BlockSpec((bt, 2, 2 * D_FF), lambda b, s: (b, 0, 0))
    return pl.pallas_call(
        functools.partial(_ffn_kernel, bt=bt, t=t, final=final),
        grid=(batch // bt, length // t),
        in_specs=[x_spec, _state_spec(st0, layer, bt)]
        + [_layer_spec(w, layer) for w in (w_up, conv_w, w_down, norm)]
        + [_const_spec(fnorm.shape)],
        out_specs=[x_spec, st_spec],
        out_shape=[jax.ShapeDtypeStruct(x.shape, F32),
                   jax.ShapeDtypeStruct(st0.shape[1:], F32)],
        scratch_shapes=[pltpu.VMEM((D_MODEL // LANES, m, LANES), F32),
                        pltpu.VMEM((m, D_MODEL), BF16),
                        pltpu.VMEM((m, D_FF), BF16)],
        compiler_params=pltpu.CompilerParams(
            dimension_semantics=("arbitrary", "arbitrary"),
            vmem_limit_bytes=VMEM_LIMIT_BYTES),
        name="ffn",
    )(x, st0, w_up, conv_w, w_down, norm, fnorm)


def _trunk(x, conv_s, gla_s, ffn_s, p):
    depth = conv_s.shape[0]
    new_c, new_g, new_f = [], [], []
    for l in range(depth):
        x, c, g = _mixer_call(x, l, conv_s, gla_s, p["w_in"], p["w_alpha"], p["b_alpha"],
                              p["conv_w"], p["g_conv"], p["g_gla"], p["w_out"], p["norm1"])
        x, f = _ffn_call(x, l, ffn_s, p["w_up"], p["ffn_conv_w"], p["w_down"],
                         p["norm2"], p["final_norm"], final=(l == depth - 1))
        new_c.append(c)
        new_g.append(g)
        new_f.append(f)
    return x, jnp.stack(new_c), jnp.stack(new_g), jnp.stack(new_f)


def kernel(x_prompt, x_sample, state_conv, state_gla, state_ffn_conv, w_in, w_alpha, b_alpha, conv_w, g_conv, g_gla, w_out, norm1, norm2, w_up, ffn_conv_w, w_down, final_norm):
    depth = w_in.shape[0]
    w_in_b = w_in.astype(BF16)
    gate_cols = w_in_b[:, :, C_R:C_R + GATE_RANK]
    w_in_r = jnp.concatenate(
        [w_in_b[:, :, :C_R], w_in_b[:, :, C_R + GATE_RANK:],
         jnp.pad(gate_cols, ((0, 0), (0, 0), (0, LANES - GATE_RANK)))], axis=-1)
    w_alpha_p = jnp.pad(w_alpha, ((0, 0), (0, LANES - GATE_RANK), (0, 0))).astype(BF16)
    p = {
        "w_in": w_in_r, "w_alpha": w_alpha_p,
        "b_alpha": b_alpha.reshape(depth, 1, D_QK),
        "conv_w": conv_w, "g_conv": g_conv.reshape(depth, 1, D_CONV),
        "g_gla": g_gla.reshape(depth, 1, HEAD_V),
        "w_out": w_out.astype(BF16),
        "norm1": norm1.reshape(depth, 1, D_MODEL), "norm2": norm2.reshape(depth, 1, D_MODEL),
        "w_up": w_up.astype(BF16), "ffn_conv_w": ffn_conv_w, "w_down": w_down.astype(BF16),
        "final_norm": final_norm.reshape(1, D_MODEL),
    }
    bp = x_prompt.shape[0]
    dt = x_prompt.dtype
    zc = jnp.zeros((depth, bp) + state_conv.shape[2:], dt)
    zg = jnp.zeros((depth, bp) + state_gla.shape[2:], dt)
    zf = jnp.zeros((depth, bp) + state_ffn_conv.shape[2:], dt)
    y_prompt, pc, pg, pf = _trunk(x_prompt, zc, zg, zf, p)
    y_sample, sc, sg, sf = _trunk(x_sample, state_conv, state_gla, state_ffn_conv, p)
    return (y_prompt, y_sample, pc, pg, pf, sc, sg, sf)
```
